```python
import math
import jax, jax.numpy as jnp
from jax import lax
import numpy as np

D_MODEL = 4096
BATCH = 1
SEQ = 8192
DEPTH = 1
DEC_BATCH = 32
DEC_SEQ = 4
PAST_LEN = 8192
PAGE_SIZE = 128

HEAD_DIM = 128
V_DIM = 2 * HEAD_DIM
N_HEADS = D_MODEL // 512
D_ATTN = N_HEADS * V_DIM
CONV_CH = D_MODEL - D_ATTN
D_MIX = D_ATTN + CONV_CH
QK_COLS = N_HEADS * 2 * HEAD_DIM
D_IN = 2 * QK_COLS + D_ATTN + 2 * CONV_CH
CONV_W = 31
D_FF = 11008
Q_BLOCK = 128
NORM_EPS = 1e-6
LN_EPS = 1e-5

kernel_name = "hybrid_diffattn_conformer_conv_step"


def lambda_init_fn(layer):
    return 0.8 - 0.6 * math.exp(-0.3 * layer)


def rms_norm(x, g):
    xf = x.astype(jnp.float32)
    y = xf * lax.rsqrt(jnp.mean(xf * xf, axis=-1, keepdims=True) + NORM_EPS)
    return (y * g.astype(jnp.float32)).astype(x.dtype)


def layer_norm(x, g, b):
    xf = x.astype(jnp.float32)
    mu = jnp.mean(xf, axis=-1, keepdims=True)
    var = jnp.mean(jnp.square(xf - mu), axis=-1, keepdims=True)
    y = (xf - mu) * lax.rsqrt(var + LN_EPS)
    return (y * g.astype(jnp.float32) + b.astype(jnp.float32)).astype(x.dtype)


def swiglu(h, w_gate, w_up, w_down):
    return (jax.nn.silu(h @ w_gate) * (h @ w_up)) @ w_down


def alibi_slopes():
    return 2.0 ** (-8.0 * jnp.arange(1, N_HEADS + 1, dtype=jnp.float32) / N_HEADS)


def diff_attention(q, q_pos, segments, lam, slopes):
    scale = HEAD_DIM ** -0.5
    scores = []
    for k, _, k_pos in segments:
        s = jnp.einsum('bqhcd,bkhcd->bhcqk', q, k).astype(jnp.float32) * scale
        dist = q_pos[:, None] - k_pos[None, :]
        bias = -slopes[:, None, None] * dist.astype(jnp.float32)[None]
        s = jnp.where((dist >= 0)[None, None, None], s + bias[None, :, None], -jnp.inf)
        scores.append(s)
    p = jax.nn.softmax(jnp.concatenate(scores, axis=-1), axis=-1)
    a = p[:, :, 0] - lam * p[:, :, 1]
    out = None
    offset = 0
    for k, v, k_pos in segments:
        n = k_pos.shape[0]
        o = jnp.einsum('bhqk,bkhe->bqhe', a[..., offset:offset + n].astype(v.dtype), v)
        out = o if out is None else out + o
        offset += n
    return out


def prompt_attention(q, k, v, lam, slopes):
    B, T = q.shape[0], q.shape[1]
    nb = T // Q_BLOCK
    pos = jnp.arange(T, dtype=jnp.int32)
    qb = q.reshape(B, nb, Q_BLOCK, N_HEADS, 2, HEAD_DIM).swapaxes(0, 1)
    pb = pos.reshape(nb, Q_BLOCK)

    def one_block(args):
        qi, pi = args
        return diff_attention(qi, pi, ((k, v, pos),), lam, slopes)

    out = lax.map(one_block, (qb, pb))
    return out.swapaxes(0, 1).reshape(B, T, N_HEADS, V_DIM)


def sample_attention(q, k, v, cache_k, cache_v, page_table, lam, slopes):
    Bd, T = q.shape[0], q.shape[1]
    past = page_table.shape[1] * cache_k.shape[1]
    k_past = cache_k[page_table].reshape(Bd, past, N_HEADS, 2, HEAD_DIM).astype(q.dtype)
    v_past = cache_v[page_table].reshape(Bd, past, N_HEADS, V_DIM).astype(v.dtype)
    past_pos = jnp.arange(past, dtype=jnp.int32)
    new_pos = past + jnp.arange(T, dtype=jnp.int32)
    return diff_attention(q, new_pos, ((k_past, v_past, past_pos), (k, v, new_pos)), lam, slopes)


def conv_module(u, hist, conv_w, conv_b, ln_g, ln_b):
    a, g = jnp.split(u, 2, axis=-1)
    glu = a * jax.nn.sigmoid(g)
    full = jnp.concatenate([hist.astype(glu.dtype), glu], axis=1)
    y = lax.conv_general_dilated(full, conv_w[:, None, :].astype(full.dtype), (1,), 'VALID',
                                 dimension_numbers=('NWC', 'WIO', 'NWC'),
                                 feature_group_count=CONV_CH) + conv_b
    y = jax.nn.silu(layer_norm(y, ln_g, ln_b))
    return y, full[:, -(CONV_W - 1):]


def layer_forward(x, conv_hist, attend, p, lam_init):
    B, T = x.shape[0], x.shape[1]
    h = rms_norm(x, p['ffn1_pre'])
    x = x + 0.5 * rms_norm(swiglu(h, p['ffn1_wg'], p['ffn1_wu'], p['ffn1_wd']), p['ffn1_post'])
    h = rms_norm(x, p['mix_pre'])
    proj = h @ p['w_in']
    q = proj[..., :QK_COLS].reshape(B, T, N_HEADS, 2, HEAD_DIM)
    k = proj[..., QK_COLS:2 * QK_COLS].reshape(B, T, N_HEADS, 2, HEAD_DIM)
    v = proj[..., 2 * QK_COLS:2 * QK_COLS + D_ATTN].reshape(B, T, N_HEADS, V_DIM)
    u = proj[..., 2 * QK_COLS + D_ATTN:]
    o = attend(q, k, v)
    o = (rms_norm(o, p['subln']) * (1.0 - lam_init)).reshape(B, T, D_ATTN)
    c, new_hist = conv_module(u, conv_hist, p['conv_w'], p['conv_b'], p['conv_ln_g'], p['conv_ln_b'])
    mix = jnp.concatenate([o, c.astype(o.dtype)], axis=-1) @ p['w_out']
    x = x + rms_norm(mix, p['mix_post'])
    h = rms_norm(x, p['ffn2_pre'])
    x = x + 0.5 * rms_norm(swiglu(h, p['ffn2_wg'], p['ffn2_wu'], p['ffn2_wd']), p['ffn2_post'])
    return x, k, v, new_hist


def setup_inputs(seed: int = 0) -> dict:
    key = jax.random.key(seed)
    ks = jax.random.split(key, 32)
    f32 = jnp.float32
    n_pages = PAST_LEN // PAGE_SIZE
    n_used = DEC_BATCH * n_pages
    n_pool = n_used + max(1, n_used // 4)

    def nrm(k, shape, scale):
        return jax.random.normal(k, shape, f32) * scale

    def gain(k, n):
        return 1.0 + nrm(k, (DEPTH, n), 0.02)

    page_table = jax.random.permutation(ks[0], n_pool)[:n_used].reshape(DEC_BATCH, n_pages).astype(jnp.int32)
    return {
        'x_prompt': nrm(ks[1], (BATCH, SEQ, D_MODEL), 1.0),
        'x_sample': nrm(ks[2], (DEC_BATCH, DEC_SEQ, D_MODEL), 1.0),
        'cache_k': nrm(ks[3], (DEPTH, n_pool, PAGE_SIZE, N_HEADS, 2, HEAD_DIM), 1.0),
        'cache_v': nrm(ks[4], (DEPTH, n_pool, PAGE_SIZE, N_HEADS, V_DIM), 1.0),
        'state_conv': nrm(ks[5], (DEPTH, DEC_BATCH, CONV_W - 1, CONV_CH), 0.5),
        'page_table': page_table,
        'ffn1_pre_g': gain(ks[6], D_MODEL),
        'ffn1_w_gate': nrm(ks[7], (DEPTH, D_MODEL, D_FF), D_MODEL ** -0.5),
        'ffn1_w_up': nrm(ks[8], (DEPTH, D_MODEL, D_FF), D_MODEL ** -0.5),
        'ffn1_w_down': nrm(ks[9], (DEPTH, D_FF, D_MODEL), D_FF ** -0.5),
        'ffn1_post_g': gain(ks[10], D_MODEL),
        'mix_pre_g': gain(ks[11], D_MODEL),
        'w_in': nrm(ks[12], (DEPTH, D_MODEL, D_IN), D_MODEL ** -0.5),
        'lambda_q1': nrm(ks[13], (DEPTH, HEAD_DIM), 0.1),
        'lambda_k1': nrm(ks[14], (DEPTH, HEAD_DIM), 0.1),
        'lambda_q2': nrm(ks[15], (DEPTH, HEAD_DIM), 0.1),
        'lambda_k2': nrm(ks[16], (DEPTH, HEAD_DIM), 0.1),
        'subln_g': gain(ks[17], V_DIM),
        'conv_w': nrm(ks[18], (DEPTH, CONV_W, CONV_CH), CONV_W ** -0.5),
        'conv_b': nrm(ks[19], (DEPTH, CONV_CH), 0.02),
        'conv_ln_g': gain(ks[20], CONV_CH),
        'conv_ln_b': nrm(ks[21], (DEPTH, CONV_CH), 0.02),
        'w_out': nrm(ks[22], (DEPTH, D_MIX, D_MODEL), D_MIX ** -0.5),
        'mix_post_g': gain(ks[23], D_MODEL),
        'ffn2_pre_g': gain(ks[24], D_MODEL),
        'ffn2_w_gate': nrm(ks[25], (DEPTH, D_MODEL, D_FF), D_MODEL ** -0.5),
        'ffn2_w_up': nrm(ks[26], (DEPTH, D_MODEL, D_FF), D_MODEL ** -0.5),
        'ffn2_w_down': nrm(ks[27], (DEPTH, D_FF, D_MODEL), D_FF ** -0.5),
        'ffn2_post_g': gain(ks[28], D_MODEL),
    }


def reference(x_prompt, x_sample, cache_k, cache_v, state_conv, page_table,
              ffn1_pre_g, ffn1_w_gate, ffn1_w_up, ffn1_w_down, ffn1_post_g,
              mix_pre_g, w_in, lambda_q1, lambda_k1, lambda_q2, lambda_k2, subln_g,
              conv_w, conv_b, conv_ln_g, conv_ln_b, w_out, mix_post_g,
              ffn2_pre_g, ffn2_w_gate, ffn2_w_up, ffn2_w_down, ffn2_post_g):
    slopes = alibi_slopes()
    y_prompt, y_sample = x_prompt, x_sample
    kp_l, vp_l, cp_l, ks_l, vs_l, cs_l = [], [], [], [], [], []
    for l in range(DEPTH):
        p = {
            'ffn1_pre': ffn1_pre_g[l], 'ffn1_wg': ffn1_w_gate[l], 'ffn1_wu': ffn1_w_up[l],
            'ffn1_wd': ffn1_w_down[l], 'ffn1_post': ffn1_post_g[l],
            'mix_pre': mix_pre_g[l], 'w_in': w_in[l], 'subln': subln_g[l],
            'conv_w': conv_w[l], 'conv_b': conv_b[l], 'conv_ln_g': conv_ln_g[l], 'conv_ln_b': conv_ln_b[l],
            'w_out': w_out[l], 'mix_post': mix_post_g[l],
            'ffn2_pre': ffn2_pre_g[l], 'ffn2_wg': ffn2_w_gate[l], 'ffn2_wu': ffn2_w_up[l],
            'ffn2_wd': ffn2_w_down[l], 'ffn2_post': ffn2_post_g[l],
        }
        lam_init = lambda_init_fn(l)
        lam = (jnp.exp(jnp.sum(lambda_q1[l].astype(jnp.float32) * lambda_k1[l].astype(jnp.float32)))
               - jnp.exp(jnp.sum(lambda_q2[l].astype(jnp.float32) * lambda_k2[l].astype(jnp.float32)))
               + lam_init)
        ck, cv = cache_k[l], cache_v[l]
        zero_hist = jnp.zeros((y_prompt.shape[0], CONV_W - 1, CONV_CH), y_prompt.dtype)
        y_prompt, kp, vp, cp = layer_forward(
            y_prompt, zero_hist,
            lambda q, k, v: prompt_attention(q, k, v, lam, slopes), p, lam_init)
        y_sample, ksm, vsm, csm = layer_forward(
            y_sample, state_conv[l],
            lambda q, k, v: sample_attention(q, k, v, ck, cv, page_table, lam, slopes), p, lam_init)
        kp_l.append(kp); vp_l.append(vp); cp_l.append(cp)
        ks_l.append(ksm); vs_l.append(vsm); cs_l.append(csm)
    k_prompt = jnp.stack(kp_l)
    v_prompt = jnp.stack(vp_l)
    conv_prompt = jnp.stack(cp_l)
    k_sample = jnp.stack(ks_l)
    v_sample = jnp.stack(vs_l)
    conv_sample = jnp.stack(cs_l)
    return (y_prompt, y_sample, k_prompt, v_prompt, conv_prompt, k_sample, v_sample, conv_sample)
```

```python
import functools
import math

import jax
import jax.numpy as jnp
from jax import lax
from jax.experimental import pallas as pl
from jax.experimental.pallas import tpu as pltpu

F32 = jnp.float32
BF16 = jnp.bfloat16

D_MODEL = 4096
HEAD_DIM = 128
V_DIM = 2 * HEAD_DIM
N_HEADS = D_MODEL // 512
D_ATTN = N_HEADS * V_DIM
CONV_CH = D_MODEL - D_ATTN
QK_COLS = N_HEADS * 2 * HEAD_DIM
CONV_W = 31
D_FF = 11008
NORM_EPS = 1e-6
LN_EPS = 1e-5
PAGE_SIZE = 128
ATTN_SCALE = HEAD_DIM ** -0.5
NEG_BIG = -1e30

FF_TILE = 1024
D_FF_PAD = -(-D_FF // FF_TILE) * FF_TILE
EPI_ROWS = 16
HALO = 32
V7X_VMEM_LIMIT = 56 * 1024 * 1024


def _lambda_init(layer):
    return 0.8 - 0.6 * math.exp(-0.3 * layer)


def _params(*sem):
    return pltpu.CompilerParams(dimension_semantics=sem, vmem_limit_bytes=V7X_VMEM_LIMIT)


def _rms(x, g):
    ms = jnp.mean(x * x, axis=-1, keepdims=True)
    return x * lax.rsqrt(ms + NORM_EPS) * g


def _prenorm_kernel(x_ref, g_ref, h_ref):
    h_ref[...] = _rms(x_ref[...], g_ref[...]).astype(BF16)


def prenorm(x, g, tm):
    m, d = x.shape
    return pl.pallas_call(
        _prenorm_kernel,
        grid=(m // tm,),
        in_specs=[pl.BlockSpec((tm, d), lambda i: (i, 0)),
                  pl.BlockSpec((1, d), lambda i: (0, 0))],
        out_specs=pl.BlockSpec((tm, d), lambda i: (i, 0)),
        out_shape=jax.ShapeDtypeStruct((m, d), BF16),
        compiler_params=_params("parallel"),
        name="prenorm",
    )(x, g)


def _gateup_kernel(h_ref, wg_ref, wu_ref, a_ref):
    h = h_ref[...]
    g = jnp.dot(h, wg_ref[...], preferred_element_type=F32)
    u = jnp.dot(h, wu_ref[...], preferred_element_type=F32)
    a_ref[...] = (g * jax.nn.sigmoid(g) * u).astype(BF16)


def gateup(h, wg, wu, tm, tn):
    m, d = h.shape
    f = wg.shape[1]
    return pl.pallas_call(
        _gateup_kernel,
        grid=(m // tm, f // tn),
        in_specs=[pl.BlockSpec((tm, d), lambda i, j: (i, 0)),
                  pl.BlockSpec((d, tn), lambda i, j: (0, j)),
                  pl.BlockSpec((d, tn), lambda i, j: (0, j))],
        out_specs=pl.BlockSpec((tm, tn), lambda i, j: (i, j)),
        out_shape=jax.ShapeDtypeStruct((m, f), BF16),
        compiler_params=_params("parallel", "arbitrary"),
        name="gateup",
    )(h, wg, wu)


def _rowmm_kernel(*refs, n_a, res_scale, has_next):
    a_refs = refs[:n_a]
    w_refs = refs[n_a:2 * n_a]
    x_ref, gpost_ref = refs[2 * n_a], refs[2 * n_a + 1]
    if has_next:
        gnext_ref, xo_ref, ho_ref = refs[2 * n_a + 2:]
    else:
        (xo_ref,) = refs[2 * n_a + 2:]
    k = pl.program_id(1)

    @pl.when(k == 0)
    def _():
        xo_ref[...] = jnp.zeros(xo_ref.shape, F32)

    for a_ref, w_ref in zip(a_refs, w_refs):
        xo_ref[...] += jnp.dot(a_ref[...], w_ref[...], preferred_element_type=F32)

    @pl.when(k == pl.num_programs(1) - 1)
    def _():
        def rows(r, carry):
            sl = pl.ds(pl.multiple_of(r * EPI_ROWS, EPI_ROWS), EPI_ROWS)
            xn = x_ref[sl, :] + res_scale * _rms(xo_ref[sl, :], gpost_ref[...])
            xo_ref[sl, :] = xn
            if has_next:
                ho_ref[sl, :] = _rms(xn, gnext_ref[...]).astype(BF16)
            return carry

        lax.fori_loop(0, xo_ref.shape[0] // EPI_ROWS, rows, 0)


def rowmm(a_list, w_list, x, g_post, g_next, res_scale, tm, tk):
    m, d = x.shape
    kdim = a_list[0].shape[1]
    n_a = len(a_list)
    has_next = g_next is not None
    in_specs = ([pl.BlockSpec((tm, tk), lambda i, k: (i, k)) for _ in a_list]
                + [pl.BlockSpec((tk, d), lambda i, k: (k, 0)) for _ in w_list]
                + [pl.BlockSpec((tm, d), lambda i, k: (i, 0)),
                   pl.BlockSpec((1, d), lambda i, k: (0, 0))])
    args = list(a_list) + list(w_list) + [x, g_post]
    out_specs = [pl.BlockSpec((tm, d), lambda i, k: (i, 0))]
    out_shape = [jax.ShapeDtypeStruct((m, d), F32)]
    if has_next:
        in_specs.append(pl.BlockSpec((1, d), lambda i, k: (0, 0)))
        args.append(g_next)
        out_specs.append(pl.BlockSpec((tm, d), lambda i, k: (i, 0)))
        out_shape.append(jax.ShapeDtypeStruct((m, d), BF16))
    outs = pl.pallas_call(
        functools.partial(_rowmm_kernel, n_a=n_a, res_scale=res_scale, has_next=has_next),
        grid=(m // tm, kdim // tk),
        in_specs=in_specs,
        out_specs=out_specs,
        out_shape=out_shape,
        compiler_params=_params("parallel", "arbitrary"),
        name="rowmm",
    )(*args)
    return outs if has_next else (outs[0], None)


def _proj_q_kernel(h_ref, w_ref, q_ref):
    q_ref[...] = jnp.dot(h_ref[...], w_ref[...], preferred_element_type=F32).astype(BF16)


def _proj_kv_kernel(h_ref, wk_ref, wv_ref, k32_ref, v32_ref, kb_ref, vb_ref):
    h = h_ref[...]
    k = jnp.dot(h, wk_ref[...], preferred_element_type=F32)
    v = jnp.dot(h, wv_ref[...], preferred_element_type=F32)
    k32_ref[...] = k
    v32_ref[...] = v
    kb_ref[...] = k.astype(BF16)
    vb_ref[...] = v.astype(BF16)


def _proj_glu_kernel(h_ref, wa_ref, wg_ref, o_ref):
    h = h_ref[...]
    a = jnp.dot(h, wa_ref[...], preferred_element_type=F32)
    g = jnp.dot(h, wg_ref[...], preferred_element_type=F32)
    o_ref[...] = a * jax.nn.sigmoid(g)


def in_proj(h, w_in, tm, tn):
    m, d = h.shape
    nq = QK_COLS // tn
    nv = D_ATTN // tn
    nc = CONV_CH // tn
    h_spec = pl.BlockSpec((tm, d), lambda i, j: (i, 0))
    o_spec = pl.BlockSpec((tm, tn), lambda i, j: (i, j))

    def w_spec(off):
        return pl.BlockSpec((d, tn), lambda i, j: (0, j + off))

    qb = pl.pallas_call(
        _proj_q_kernel, grid=(m // tm, nq),
        in_specs=[h_spec, w_spec(0)], out_specs=o_spec,
        out_shape=jax.ShapeDtypeStruct((m, QK_COLS), BF16),
        compiler_params=_params("parallel", "arbitrary"), name="proj_q",
    )(h, w_in)
    k32, v32, kb, vb = pl.pallas_call(
        _proj_kv_kernel, grid=(m // tm, nv),
        in_specs=[h_spec, w_spec(nq), w_spec(2 * nq)], out_specs=[o_spec] * 4,
        out_shape=[jax.ShapeDtypeStruct((m, QK_COLS), F32), jax.ShapeDtypeStruct((m, D_ATTN), F32),
                   jax.ShapeDtypeStruct((m, QK_COLS), BF16), jax.ShapeDtypeStruct((m, D_ATTN), BF16)],
        compiler_params=_params("parallel", "arbitrary"), name="proj_kv",
    )(h, w_in, w_in)
    glu = pl.pallas_call(
        _proj_glu_kernel, grid=(m // tm, nc),
        in_specs=[h_spec, w_spec(2 * nq + nv), w_spec(2 * nq + nv + nc)], out_specs=o_spec,
        out_shape=jax.ShapeDtypeStruct((m, CONV_CH), F32),
        compiler_params=_params("parallel", "arbitrary"), name="proj_glu",
    )(h, w_in, w_in)
    return qb, k32, v32, kb, vb, glu


def _lambda_full(lam_ref, lam_init):
    lv = lam_ref[...]
    e1 = jnp.exp(jnp.sum(lv[0:1] * lv[1:2], axis=-1, keepdims=True))
    e2 = jnp.exp(jnp.sum(lv[2:3] * lv[3:4], axis=-1, keepdims=True))
    return e1 - e2 + lam_init


def _softmax_step(s, vb, m_ref, l_ref, acc_ref):
    m_old = m_ref[...]
    m_new = jnp.maximum(m_old, jnp.max(s, axis=-1, keepdims=True))
    alpha = jnp.exp(m_old - m_new)
    p = jnp.exp(s - m_new)
    l_ref[...] = alpha * l_ref[...] + jnp.sum(p, axis=-1, keepdims=True)
    acc_ref[...] = alpha * acc_ref[...] + jnp.dot(p.astype(BF16), vb, preferred_element_type=F32)
    m_ref[...] = m_new


def _prompt_attn_kernel(slopes_ref, lam_ref, subg_ref, q_ref, k_ref, v_ref, o_ref,
                        m_ref, l_ref, acc_ref, *, tq, lam_init):
    h = pl.program_id(0)
    qi = pl.program_id(1)
    slope = slopes_ref[h]
    m_ref[...] = jnp.full(m_ref.shape, NEG_BIG, F32)
    l_ref[...] = jnp.zeros(l_ref.shape, F32)
    acc_ref[...] = jnp.zeros(acc_ref.shape, F32)
    q = q_ref[...]
    rel_local = (lax.broadcasted_iota(jnp.int32, (tq, tq), 0)
                 - lax.broadcasted_iota(jnp.int32, (tq, tq), 1))

    def block(kj, masked):
        k0 = pl.multiple_of(kj * tq, tq)
        kb = k_ref[pl.ds(k0, tq), :]
        vb = v_ref[pl.ds(k0, tq), :]
        rel = rel_local + (qi - kj) * tq
        bias = -slope * rel.astype(F32)
        for c in range(2):
            s = lax.dot_general(q[:, c * HEAD_DIM:(c + 1) * HEAD_DIM], kb[:, c * HEAD_DIM:(c + 1) * HEAD_DIM],
                                (((1,), (1,)), ((), ())), preferred_element_type=F32)
            s = s * ATTN_SCALE + bias
            if masked:
                s = jnp.where(rel >= 0, s, NEG_BIG)
            _softmax_step(s, vb, m_ref.at[c], l_ref.at[c], acc_ref.at[c])

    def body(kj, carry):
        block(kj, False)
        return carry

    lax.fori_loop(0, qi, body, 0)
    block(qi, True)

    lam = _lambda_full(lam_ref, lam_init)
    o = acc_ref[0] / l_ref[0] - lam * (acc_ref[1] / l_ref[1])
    o = _rms(o, subg_ref[...]) * (1.0 - lam_init)
    o_ref[...] = o.astype(BF16)


def prompt_attention(qb, kb, vb, slopes, lam_vecs, subg, lam_init, tq):
    t = qb.shape[0]
    return pl.pallas_call(
        functools.partial(_prompt_attn_kernel, tq=tq, lam_init=lam_init),
        grid=(N_HEADS, t // tq),
        in_specs=[pl.BlockSpec(memory_space=pltpu.SMEM),
                  pl.BlockSpec((4, HEAD_DIM), lambda h, i: (0, 0)),
                  pl.BlockSpec((1, V_DIM), lambda h, i: (0, 0)),
                  pl.BlockSpec((tq, V_DIM), lambda h, i: (i, h)),
                  pl.BlockSpec((t, V_DIM), lambda h, i: (0, h)),
                  pl.BlockSpec((t, V_DIM), lambda h, i: (0, h))],
        out_specs=pl.BlockSpec((tq, V_DIM), lambda h, i: (i, h)),
        out_shape=jax.ShapeDtypeStruct((t, D_ATTN), BF16),
        scratch_shapes=[pltpu.VMEM((2, tq, 1), F32), pltpu.VMEM((2, tq, 1), F32),
                        pltpu.VMEM((2, tq, V_DIM), F32)],
        compiler_params=_params("parallel", "arbitrary"),
        name="prompt_attn",
    )(slopes, lam_vecs, subg, qb, kb, vb)


def _sample_attn_kernel(pt_ref, qbd_ref, srow_ref, qpos_ref, lam_ref, subg_ref, knew_ref, vnew_ref, *rest,
                        n_pages, n_new, past_len, lam_init):
    k_refs = rest[:n_pages]
    v_refs = rest[n_pages:2 * n_pages]
    o_ref, m_ref, l_ref, acc_ref = rest[2 * n_pages:]
    g = pl.program_id(1)
    nt = (((1,), (1,)), ((), ()))

    @pl.when(g == 0)
    def _():
        m_ref[...] = jnp.full(m_ref.shape, NEG_BIG, F32)
        l_ref[...] = jnp.zeros(l_ref.shape, F32)
        acc_ref[...] = jnp.zeros(acc_ref.shape, F32)

    q = qbd_ref[0]
    srow = srow_ref[...]
    qpos = qpos_ref[...]
    lane = lax.broadcasted_iota(jnp.int32, (1, PAGE_SIZE), 1)
    for p in range(n_pages):
        kb = k_refs[p][0].astype(BF16)
        vb = v_refs[p][0].astype(BF16)
        kpos = ((g * n_pages + p) * PAGE_SIZE + lane).astype(F32)
        s = lax.dot_general(q, kb, nt, preferred_element_type=F32) * ATTN_SCALE
        s = s - srow * (qpos - kpos)
        _softmax_step(s, vb, m_ref, l_ref, acc_ref)

    @pl.when(g == pl.num_programs(1) - 1)
    def _():
        n_pad = knew_ref.shape[1]
        t_new = lax.broadcasted_iota(jnp.int32, (1, n_pad), 1)
        kpos = (past_len + t_new).astype(F32)
        s = lax.dot_general(q, knew_ref[0], nt, preferred_element_type=F32) * ATTN_SCALE
        s = s - srow * (qpos - kpos)
        s = jnp.where(kpos <= qpos, s, NEG_BIG)
        _softmax_step(s, vnew_ref[0], m_ref, l_ref, acc_ref)

        lam = _lambda_full(lam_ref, lam_init)
        half = n_new * N_HEADS
        pn = acc_ref[...] / l_ref[...]
        d = pn[:half] - lam * pn[half:]
        own = (lax.broadcasted_iota(jnp.int32, (N_HEADS, D_ATTN), 1) // V_DIM
               == lax.broadcasted_iota(jnp.int32, (N_HEADS, D_ATTN), 0))
        rows = []
        for t in range(n_new):
            dt = jnp.where(own, d[t * N_HEADS:(t + 1) * N_HEADS], 0.0)
            ms = jnp.sum(dt * dt, axis=-1, keepdims=True) * (1.0 / V_DIM)
            dn = dt * lax.rsqrt(ms + NORM_EPS)
            rows.append(jnp.sum(dn, axis=0, keepdims=True) * subg_ref[...] * (1.0 - lam_init))
        o_ref[0] = jnp.concatenate(rows, axis=0)


def sample_attention(qbd, srow, qpos, lam_vecs, subg_tiled, knew, vnew, cache_k, cache_v, page_table,
                     lam_init, n_new, n_pages):
    nb, nrow, _ = qbd.shape
    pages_per_seq = page_table.shape[1]
    past_len = pages_per_seq * PAGE_SIZE
    n_pad = knew.shape[1]

    def page_spec(p):
        return pl.BlockSpec((1, PAGE_SIZE, QK_COLS), lambda b, g, pt: (pt[b, g * n_pages + p], 0, 0))

    const2 = lambda b, g, pt: (0, 0)
    per_b = lambda b, g, pt: (b, 0, 0)
    grid_spec = pltpu.PrefetchScalarGridSpec(
        num_scalar_prefetch=1,
        grid=(nb, pages_per_seq // n_pages),
        in_specs=[pl.BlockSpec((1, nrow, QK_COLS), per_b),
                  pl.BlockSpec((nrow, 1), const2),
                  pl.BlockSpec((nrow, 1), const2),
                  pl.BlockSpec((4, HEAD_DIM), const2),
                  pl.BlockSpec((1, D_ATTN), const2),
                  pl.BlockSpec((1, n_pad, QK_COLS), per_b),
                  pl.BlockSpec((1, n_pad, D_ATTN), per_b)]
                 + [page_spec(p) for p in range(n_pages)]
                 + [page_spec(p) for p in range(n_pages)],
        out_specs=pl.BlockSpec((1, n_new, D_ATTN), per_b),
        scratch_shapes=[pltpu.VMEM((nrow, 1), F32), pltpu.VMEM((nrow, 1), F32),
                        pltpu.VMEM((nrow, D_ATTN), F32)],
    )
    return pl.pallas_call(
        functools.partial(_sample_attn_kernel, n_pages=n_pages, n_new=n_new, past_len=past_len,
                          lam_init=lam_init),
        grid_spec=grid_spec,
        out_shape=jax.ShapeDtypeStruct((nb, n_new, D_ATTN), F32),
        compiler_params=_params("parallel", "arbitrary"),
        name="sample_attn",
    )(page_table, qbd, srow, qpos, lam_vecs, subg_tiled, knew, vnew,
      *([cache_k] * n_pages), *([cache_v] * n_pages))


def _ln_swish(y, g, b):
    mu = jnp.mean(y, axis=-1, keepdims=True)
    yc = y - mu
    var = jnp.mean(yc * yc, axis=-1, keepdims=True)
    z = yc * lax.rsqrt(var + LN_EPS) * g + b
    return z * jax.nn.sigmoid(z)


def _prompt_conv_kernel(hist_ref, halo_ref, cur_ref, cw_ref, cb_ref, lg_ref, lb_ref, o_ref, buf_ref, y_ref,
                        *, tt, rb, lanes):
    i = pl.program_id(0)
    use_prev = jnp.broadcast_to(i > 0, halo_ref.shape)
    buf_ref[0:HALO, :] = jnp.where(use_prev, halo_ref[...], hist_ref[...])
    buf_ref[HALO:HALO + tt, :] = cur_ref[...]
    shift = HALO - (CONV_W - 1)

    def chunk(cc, carry):
        c0 = pl.multiple_of(cc * lanes, lanes)
        cols = pl.ds(c0, lanes)
        for r0 in range(0, tt, rb):
            acc = jnp.broadcast_to(cb_ref[:, cols], (rb, lanes))
            for w in range(CONV_W):
                acc = acc + buf_ref[pl.ds(r0 + w + shift, rb), cols] * cw_ref[pl.ds(w, 1), cols]
            y_ref[pl.ds(r0, rb), cols] = acc
        return carry

    lax.fori_loop(0, CONV_CH // lanes, chunk, 0)
    o_ref[...] = _ln_swish(y_ref[...], lg_ref[...], lb_ref[...]).astype(BF16)


def prompt_conv(glu, hist, cw, cb, lg, lb, tt):
    t, c = glu.shape
    const = lambda i: (0, 0)
    return pl.pallas_call(
        functools.partial(_prompt_conv_kernel, tt=tt, rb=64, lanes=128),
        grid=(t // tt,),
        in_specs=[pl.BlockSpec((HALO, c), const),
                  pl.BlockSpec((HALO, c), lambda i: (jnp.maximum(i * (tt // HALO) - 1, 0), 0)),
                  pl.BlockSpec((tt, c), lambda i: (i, 0)),
                  pl.BlockSpec((CONV_W, c), const),
                  pl.BlockSpec((1, c), const), pl.BlockSpec((1, c), const), pl.BlockSpec((1, c), const)],
        out_specs=pl.BlockSpec((tt, c), lambda i: (i, 0)),
        out_shape=jax.ShapeDtypeStruct((t, c), BF16),
        scratch_shapes=[pltpu.VMEM((HALO + tt, c), F32), pltpu.VMEM((tt, c), F32)],
        compiler_params=_params("arbitrary"),
        name="prompt_conv",
    )(hist, glu, glu, cw, cb, lg, lb)


def _sample_conv_kernel(hist_ref, glu_ref, cw_ref, cb_ref, lg_ref, lb_ref, o_ref, nh_ref, full_ref, *, n_new):
    nh = CONV_W - 1
    full_ref[0:nh, :] = hist_ref[0]
    full_ref[nh:nh + n_new, :] = glu_ref[0]
    acc = jnp.broadcast_to(cb_ref[...], (n_new, cb_ref.shape[1]))
    for w in range(CONV_W):
        acc = acc + full_ref[w:w + n_new, :] * cw_ref[w:w + 1, :]
    o_ref[0] = _ln_swish(acc, lg_ref[...], lb_ref[...])
    nh_ref[0] = full_ref[n_new:n_new + nh, :]


def sample_conv(glu, hist, cw, cb, lg, lb):
    nb, n_new, c = glu.shape
    nh = CONV_W - 1
    const = lambda b: (0, 0)
    per_b = lambda b: (b, 0, 0)
    return pl.pallas_call(
        functools.partial(_sample_conv_kernel, n_new=n_new),
        grid=(nb,),
        in_specs=[pl.BlockSpec((1, nh, c), per_b), pl.BlockSpec((1, n_new, c), per_b),
                  pl.BlockSpec((CONV_W, c), const),
                  pl.BlockSpec((1, c), const), pl.BlockSpec((1, c), const), pl.BlockSpec((1, c), const)],
        out_specs=[pl.BlockSpec((1, n_new, c), per_b), pl.BlockSpec((1, nh, c), per_b)],
        out_shape=[jax.ShapeDtypeStruct((nb, n_new, c), F32), jax.ShapeDtypeStruct((nb, nh, c), F32)],
        scratch_shapes=[pltpu.VMEM((nh + n_new + 6, c), F32)],
        compiler_params=_params("parallel"),
        name="sample_conv",
    )(hist, glu, cw, cb, lg, lb)


def _block_diag_queries(qb, nb, n_new):
    q = qb.reshape(nb, n_new, N_HEADS, 2, HEAD_DIM).transpose(0, 3, 1, 2, 4)
    eye_h = jnp.eye(N_HEADS, dtype=qb.dtype)
    eye_c = jnp.eye(2, dtype=qb.dtype)
    qbd = (q[:, :, :, :, None, None, :] * eye_h[None, None, None, :, :, None, None]
           * eye_c[None, :, None, None, None, :, None])
    return qbd.reshape(nb, 2 * n_new * N_HEADS, QK_COLS)


def _layer(x_p, x_s, ck, cv, hist_s, page_table, w, lam_init, slopes):
    t_p = x_p.shape[0]
    nb, n_new = hist_s.shape[0], x_s.shape[0] // hist_s.shape[0]
    tm_p, tm_s = 512, x_s.shape[0]
    outs = {}
    for name, x, tm, tm_big in (("p", x_p, tm_p, 1024), ("s", x_s, tm_s, tm_s)):
        h = prenorm(x, w["ffn1_pre"], tm)
        a = gateup(h, w["ffn1_wg"], w["ffn1_wu"], tm_big, 512)
        x, h = rowmm([a], [w["ffn1_wd"]], x, w["ffn1_post"], w["mix_pre"], 0.5, tm, 512)
        qb, k32, v32, kb, vb, glu = in_proj(h, w["w_in"], tm_big, 512)
        if name == "p":
            o = prompt_attention(qb, kb, vb, slopes, w["lam_vecs"], w["subln"], lam_init, 512)
            hist0 = jnp.zeros((HALO, CONV_CH), F32)
            c = prompt_conv(glu, hist0, w["conv_w"], w["conv_b"], w["conv_ln_g"], w["conv_ln_b"], 256)
            new_hist = glu[t_p - (CONV_W - 1):][None]
        else:
            qbd = _block_diag_queries(qb, nb, n_new)
            r = jnp.arange(2 * n_new * N_HEADS)
            srow = slopes[r % N_HEADS][:, None]
            pages_per_seq = page_table.shape[1]
            qpos = (pages_per_seq * PAGE_SIZE + (r // N_HEADS) % n_new).astype(F32)[:, None]
            pad = ((0, 0), (0, 16 - n_new), (0, 0))
            knew = jnp.pad(kb.reshape(nb, n_new, QK_COLS), pad)
            vnew = jnp.pad(vb.reshape(nb, n_new, D_ATTN), pad)
            subg_tiled = jnp.tile(w["subln"], (1, N_HEADS))
            o = sample_attention(qbd, srow, qpos, w["lam_vecs"], subg_tiled, knew, vnew,
                                 ck.reshape(ck.shape[0], PAGE_SIZE, QK_COLS),
                                 cv.reshape(cv.shape[0], PAGE_SIZE, D_ATTN),
                                 page_table, lam_init, n_new, 4)
            o = o.reshape(nb * n_new, D_ATTN).astype(BF16)
            c, new_hist = sample_conv(glu.reshape(nb, n_new, CONV_CH), hist_s, w["conv_w"], w["conv_b"],
                                      w["conv_ln_g"], w["conv_ln_b"])
            c = c.reshape(nb * n_new, CONV_CH).astype(BF16)
        x, h = rowmm([o, c], [w["w_out_a"], w["w_out_c"]], x, w["mix_post"], w["ffn2_pre"], 1.0, tm, 256)
        a = gateup(h, w["ffn2_wg"], w["ffn2_wu"], tm_big, 512)
        x, _ = rowmm([a], [w["ffn2_wd"]], x, w["ffn2_post"], None, 0.5, tm, 512)
        outs[name] = (x, k32, v32, new_hist)
    return outs


def kernel(x_prompt, x_sample, cache_k, cache_v, state_conv, page_table, ffn1_pre_g, ffn1_w_gate, ffn1_w_up, ffn1_w_down, ffn1_post_g, mix_pre_g, w_in, lambda_q1, lambda_k1, lambda_q2, lambda_k2, subln_g, conv_w, conv_b, conv_ln_g, conv_ln_b, w_out, mix_post_g, ffn2_pre_g, ffn2_w_gate, ffn2_w_up, ffn2_w_down, ffn2_post_g):
    depth = w_in.shape[0]
    batch, seq, _ = x_prompt.shape
    nb, n_new, _ = x_sample.shape
    slopes = 2.0 ** (-8.0 * jnp.arange(1, N_HEADS + 1, dtype=F32) / N_HEADS)
    ff_pad = D_FF_PAD - D_FF

    def up_w(wt):
        return jnp.pad(wt, ((0, 0), (0, ff_pad))).astype(BF16)

    def down_w(wt):
        return jnp.pad(wt, ((0, ff_pad), (0, 0))).astype(BF16)

    row = lambda v: v.reshape(1, -1).astype(F32)
    assert batch == 1
    x_p = x_prompt.reshape(seq, D_MODEL)
    x_s = x_sample.reshape(nb * n_new, D_MODEL)
    kp, vp, cp, ks, vs, cs = [], [], [], [], [], []
    for l in range(depth):
        wo = w_out[l].astype(BF16)
        w = {
            "ffn1_pre": row(ffn1_pre_g[l]), "ffn1_wg": up_w(ffn1_w_gate[l]), "ffn1_wu": up_w(ffn1_w_up[l]),
            "ffn1_wd": down_w(ffn1_w_down[l]), "ffn1_post": row(ffn1_post_g[l]),
            "mix_pre": row(mix_pre_g[l]), "w_in": w_in[l].astype(BF16), "subln": row(subln_g[l]),
            "lam_vecs": jnp.stack([lambda_q1[l], lambda_k1[l], lambda_q2[l], lambda_k2[l]]).astype(F32),
            "conv_w": conv_w[l].astype(F32), "conv_b": row(conv_b[l]),
            "conv_ln_g": row(conv_ln_g[l]), "conv_ln_b": row(conv_ln_b[l]),
            "w_out_a": wo[:D_ATTN], "w_out_c": wo[D_ATTN:], "mix_post": row(mix_post_g[l]),
            "ffn2_pre": row(ffn2_pre_g[l]), "ffn2_wg": up_w(ffn2_w_gate[l]), "ffn2_wu": up_w(ffn2_w_up[l]),
            "ffn2_wd": down_w(ffn2_w_down[l]), "ffn2_post": row(ffn2_post_g[l]),
        }
        outs = _layer(x_p, x_s, cache_k[l], cache_v[l], state_conv[l], page_table, w, _lambda_init(l), slopes)
        x_p, k32, v32, nh = outs["p"]
        kp.append(k32.reshape(batch, seq, N_HEADS, 2, HEAD_DIM))
        vp.append(v32.reshape(batch, seq, N_HEADS, V_DIM))
        cp.append(nh)
        x_s, k32, v32, nh = outs["s"]
        ks.append(k32.reshape(nb, n_new, N_HEADS, 2, HEAD_DIM))
        vs.append(v32.reshape(nb, n_new, N_HEADS, V_DIM))
        cs.append(nh)
    return (x_p.reshape(batch, seq, D_MODEL), x_s.reshape(nb, n_new, D_MODEL),
            jnp.stack(kp), jnp.stack(vp), jnp.stack(cp), jnp.stack(ks), jnp.stack(vs), jnp.stack(cs))
```

```python
import functools
import math

import jax
import jax.numpy as jnp
from jax import lax
from jax.experimental import pallas as pl
from jax.experimental.pallas import tpu as pltpu

F32 = jnp.float32
BF16 = jnp.bfloat16

D_MODEL = 4096
HEAD_DIM = 128
V_DIM = 2 * HEAD_DIM
N_HEADS = D_MODEL // 512
D_ATTN = N_HEADS * V_DIM
CONV_CH = D_MODEL - D_ATTN
QK_COLS = N_HEADS * 2 * HEAD_DIM
CONV_W = 31
D_FF = 11008
NORM_EPS = 1e-6
LN_EPS = 1e-5
PAGE_SIZE = 128
ATTN_SCALE = HEAD_DIM ** -0.5
LOG2E = math.log2(math.e)
NEG_BIG = -1e30
SA_COLS = 128

FF_TILE = 1024
D_FF_PAD = -(-D_FF // FF_TILE) * FF_TILE
ROW_CHUNK = 64
EPI_ROWS = 16
HALO = 32
V7X_VMEM_LIMIT = 56 * 1024 * 1024


def _lambda_init(layer):
    return 0.8 - 0.6 * math.exp(-0.3 * layer)


def _params(*sem):
    return pltpu.CompilerParams(dimension_semantics=sem, vmem_limit_bytes=V7X_VMEM_LIMIT)


def _rms(x, g):
    ms = jnp.mean(x * x, axis=-1, keepdims=True)
    return x * lax.rsqrt(ms + NORM_EPS) * g


def _prenorm_kernel(x_ref, g_ref, h_ref):
    h_ref[...] = _rms(x_ref[...], g_ref[...]).astype(BF16)


def prenorm(x, g, tm):
    m, d = x.shape
    return pl.pallas_call(
        _prenorm_kernel,
        grid=(m // tm,),
        in_specs=[pl.BlockSpec((tm, d), lambda i: (i, 0)),
                  pl.BlockSpec((1, d), lambda i: (0, 0))],
        out_specs=pl.BlockSpec((tm, d), lambda i: (i, 0)),
        out_shape=jax.ShapeDtypeStruct((m, d), BF16),
        compiler_params=_params("parallel"),
        name="prenorm",
    )(x, g)


def _gateup_kernel(h_ref, wg_ref, wu_ref, a_ref):
    h = h_ref[...]
    g = jnp.dot(h, wg_ref[...], preferred_element_type=F32)
    u = jnp.dot(h, wu_ref[...], preferred_element_type=F32)
    a_ref[...] = (g * jax.nn.sigmoid(g) * u).astype(BF16)


def gateup(h, wg, wu, tm, tn):
    m, d = h.shape
    f = wg.shape[1]
    return pl.pallas_call(
        _gateup_kernel,
        grid=(m // tm, f // tn),
        in_specs=[pl.BlockSpec((tm, d), lambda i, j: (i, 0)),
                  pl.BlockSpec((d, tn), lambda i, j: (0, j)),
                  pl.BlockSpec((d, tn), lambda i, j: (0, j))],
        out_specs=pl.BlockSpec((tm, tn), lambda i, j: (i, j)),
        out_shape=jax.ShapeDtypeStruct((m, f), BF16),
        compiler_params=_params("parallel", "arbitrary"),
        name="gateup",
    )(h, wg, wu)


def _rowmm_kernel(*refs, n_a, res_scale, has_next):
    a_refs = refs[:n_a]
    w_refs = refs[n_a:2 * n_a]
    x_ref, gpost_ref = refs[2 * n_a], refs[2 * n_a + 1]
    if has_next:
        gnext_ref, xo_ref, ho_ref = refs[2 * n_a + 2:]
    else:
        (xo_ref,) = refs[2 * n_a + 2:]
    k = pl.program_id(1)

    @pl.when(k == 0)
    def _():
        xo_ref[...] = jnp.zeros(xo_ref.shape, F32)

    for a_ref, w_ref in zip(a_refs, w_refs):
        xo_ref[...] += jnp.dot(a_ref[...], w_ref[...], preferred_element_type=F32)

    @pl.when(k == pl.num_programs(1) - 1)
    def _():
        def rows(r, carry):
            sl = pl.ds(pl.multiple_of(r * EPI_ROWS, EPI_ROWS), EPI_ROWS)
            xn = x_ref[sl, :] + res_scale * _rms(xo_ref[sl, :], gpost_ref[...])
            xo_ref[sl, :] = xn
            if has_next:
                ho_ref[sl, :] = _rms(xn, gnext_ref[...]).astype(BF16)
            return carry

        lax.fori_loop(0, xo_ref.shape[0] // EPI_ROWS, rows, 0)


def rowmm(a_list, w_list, x, g_post, g_next, res_scale, tm, tk):
    m, d = x.shape
    kdim = a_list[0].shape[1]
    n_a = len(a_list)
    has_next = g_next is not None
    in_specs = ([pl.BlockSpec((tm, tk), lambda i, k: (i, k)) for _ in a_list]
                + [pl.BlockSpec((tk, d), lambda i, k: (k, 0)) for _ in w_list]
                + [pl.BlockSpec((tm, d), lambda i, k: (i, 0)),
                   pl.BlockSpec((1, d), lambda i, k: (0, 0))])
    args = list(a_list) + list(w_list) + [x, g_post]
    out_specs = [pl.BlockSpec((tm, d), lambda i, k: (i, 0))]
    out_shape = [jax.ShapeDtypeStruct((m, d), F32)]
    if has_next:
        in_specs.append(pl.BlockSpec((1, d), lambda i, k: (0, 0)))
        args.append(g_next)
        out_specs.append(pl.BlockSpec((tm, d), lambda i, k: (i, 0)))
        out_shape.append(jax.ShapeDtypeStruct((m, d), BF16))
    outs = pl.pallas_call(
        functools.partial(_rowmm_kernel, n_a=n_a, res_scale=res_scale, has_next=has_next),
        grid=(m // tm, kdim // tk),
        in_specs=in_specs,
        out_specs=out_specs,
        out_shape=out_shape,
        compiler_params=_params("parallel", "arbitrary"),
        name="rowmm",
    )(*args)
    return outs if has_next else (outs[0], None)


def _proj_q_kernel(h_ref, w_ref, q_ref):
    q = jnp.dot(h_ref[...], w_ref[...], preferred_element_type=F32)
    q_ref[...] = (q * (ATTN_SCALE * LOG2E)).astype(BF16)


def _proj_kv_kernel(h_ref, wk_ref, wv_ref, k32_ref, v32_ref, kb_ref, vb_ref):
    h = h_ref[...]
    k = jnp.dot(h, wk_ref[...], preferred_element_type=F32)
    v = jnp.dot(h, wv_ref[...], preferred_element_type=F32)
    k32_ref[...] = k
    v32_ref[...] = v
    kb_ref[...] = k.astype(BF16)
    vb_ref[...] = v.astype(BF16)


def _proj_glu_kernel(h_ref, wa_ref, wg_ref, o_ref):
    h = h_ref[...]
    a = jnp.dot(h, wa_ref[...], preferred_element_type=F32)
    g = jnp.dot(h, wg_ref[...], preferred_element_type=F32)
    o_ref[...] = a * jax.nn.sigmoid(g)


def in_proj(h, w_in, tm, tn):
    m, d = h.shape
    nq = QK_COLS // tn
    nv = D_ATTN // tn
    nc = CONV_CH // tn
    h_spec = pl.BlockSpec((tm, d), lambda i, j: (i, 0))
    o_spec = pl.BlockSpec((tm, tn), lambda i, j: (i, j))

    def w_spec(off):
        return pl.BlockSpec((d, tn), lambda i, j: (0, j + off))

    qb = pl.pallas_call(
        _proj_q_kernel, grid=(m // tm, nq),
        in_specs=[h_spec, w_spec(0)], out_specs=o_spec,
        out_shape=jax.ShapeDtypeStruct((m, QK_COLS), BF16),
        compiler_params=_params("parallel", "arbitrary"), name="proj_q",
    )(h, w_in)
    k32, v32, kb, vb = pl.pallas_call(
        _proj_kv_kernel, grid=(m // tm, nv),
        in_specs=[h_spec, w_spec(nq), w_spec(2 * nq)], out_specs=[o_spec] * 4,
        out_shape=[jax.ShapeDtypeStruct((m, QK_COLS), F32), jax.ShapeDtypeStruct((m, D_ATTN), F32),
                   jax.ShapeDtypeStruct((m, QK_COLS), BF16), jax.ShapeDtypeStruct((m, D_ATTN), BF16)],
        compiler_params=_params("parallel", "arbitrary"), name="proj_kv",
    )(h, w_in, w_in)
    glu = pl.pallas_call(
        _proj_glu_kernel, grid=(m // tm, nc),
        in_specs=[h_spec, w_spec(2 * nq + nv), w_spec(2 * nq + nv + nc)], out_specs=o_spec,
        out_shape=jax.ShapeDtypeStruct((m, CONV_CH), F32),
        compiler_params=_params("parallel", "arbitrary"), name="proj_glu",
    )(h, w_in, w_in)
    return qb, k32, v32, kb, vb, glu


def _lambda_full(lam_ref, lam_init):
    lv = lam_ref[...]
    e1 = jnp.exp(jnp.sum(lv[0:1] * lv[1:2], axis=-1, keepdims=True))
    e2 = jnp.exp(jnp.sum(lv[2:3] * lv[3:4], axis=-1, keepdims=True))
    return e1 - e2 + lam_init


def _prompt_attn_kernel(slopes_ref, lam_ref, subg_ref, q_ref, k_ref, v_ref, o_ref,
                        b0_ref, s_ref, p_ref, alpha_ref, m_ref, l_ref, acc_ref, *, tq, tk, lam_init):
    h = pl.program_id(0)
    qi = pl.program_id(1)
    slope2 = slopes_ref[h] * LOG2E
    nt = (((1,), (1,)), ((), ()))
    per_q = tq // tk

    def rel_local(n_rows):
        return lax.broadcasted_iota(jnp.int32, (n_rows, tk), 0) - lax.broadcasted_iota(jnp.int32, (n_rows, tk), 1)

    b0_ref[...] = -slope2 * rel_local(tq).astype(F32)
    m_ref[...] = jnp.full(m_ref.shape, NEG_BIG, F32)
    l_ref[...] = jnp.zeros(l_ref.shape, F32)
    acc_ref[...] = jnp.zeros(acc_ref.shape, F32)

    def scores(kj):
        kb = k_ref[pl.ds(pl.multiple_of(kj * tk, tk), tk), :]
        for c in range(2):
            cols = slice(c * HEAD_DIM, (c + 1) * HEAD_DIM)
            s_ref[kj % 2, c] = lax.dot_general(q_ref[:, cols], kb[:, cols], nt, preferred_element_type=F32)

    def consume(kj, diag_offset):
        vb = v_ref[pl.ds(pl.multiple_of(kj * tk, tk), tk), :]
        c_blk = -slope2 * (qi * tq - kj * tk).astype(F32)
        for c in range(2):
            for r in range(0, tq, ROW_CHUNK):
                rows = slice(r, r + ROW_CHUNK)
                s = s_ref[kj % 2, c, rows, :] + b0_ref[rows, :]
                if diag_offset is not None:
                    s = jnp.where(rel_local(ROW_CHUNK) >= diag_offset - r, s, NEG_BIG)
                m_prev = m_ref[c, rows, :]
                m_next = jnp.maximum(m_prev, jnp.max(s, axis=1, keepdims=True) + c_blk)
                alpha = jnp.exp2(m_prev - m_next)
                p = jnp.exp2(s - pltpu.repeat(m_next - c_blk, tk // 128, axis=1))
                l_ref[c, rows, :] = alpha * l_ref[c, rows, :] + jnp.sum(p, axis=1, keepdims=True)
                m_ref[c, rows, :] = m_next
                alpha_ref[c, rows, :] = alpha
                p_ref[c, rows, :] = p.astype(BF16)
            acc_ref[c] = (pltpu.repeat(alpha_ref[c], V_DIM // 128, axis=1) * acc_ref[c]
                          + jnp.dot(p_ref[c], vb, preferred_element_type=F32))

    n_full = qi * per_q
    scores(0)

    def body(kj, carry):
        consume(kj, None)
        scores(kj + 1)
        return carry

    lax.fori_loop(0, n_full, body, 0)
    for d in range(per_q):
        if d + 1 < per_q:
            scores(n_full + d + 1)
        consume(n_full + d, d * tk)

    lam = _lambda_full(lam_ref, lam_init)
    o = (acc_ref[0] / pltpu.repeat(l_ref[0], V_DIM // 128, axis=1)
         - lam * (acc_ref[1] / pltpu.repeat(l_ref[1], V_DIM // 128, axis=1)))
    o_ref[...] = (_rms(o, subg_ref[...]) * (1.0 - lam_init)).astype(BF16)


def prompt_attention(qb, kb, vb, slopes, lam_vecs, subg, lam_init, tq, tk):
    t = qb.shape[0]
    return pl.pallas_call(
        functools.partial(_prompt_attn_kernel, tq=tq, tk=tk, lam_init=lam_init),
        grid=(N_HEADS, t // tq),
        in_specs=[pl.BlockSpec(memory_space=pltpu.SMEM),
                  pl.BlockSpec((4, HEAD_DIM), lambda h, i: (0, 0)),
                  pl.BlockSpec((1, V_DIM), lambda h, i: (0, 0)),
                  pl.BlockSpec((tq, V_DIM), lambda h, i: (i, h)),
                  pl.BlockSpec((t, V_DIM), lambda h, i: (0, h)),
                  pl.BlockSpec((t, V_DIM), lambda h, i: (0, h))],
        out_specs=pl.BlockSpec((tq, V_DIM), lambda h, i: (i, h)),
        out_shape=jax.ShapeDtypeStruct((t, D_ATTN), BF16),
        scratch_shapes=[pltpu.VMEM((tq, tk), F32), pltpu.VMEM((2, 2, tq, tk), F32),
                        pltpu.VMEM((2, tq, tk), BF16), pltpu.VMEM((2, tq, 128), F32),
                        pltpu.VMEM((2, tq, 128), F32), pltpu.VMEM((2, tq, 128), F32),
                        pltpu.VMEM((2, tq, V_DIM), F32)],
        compiler_params=_params("parallel", "arbitrary"),
        name="prompt_attn",
    )(slopes, lam_vecs, subg, qb, kb, vb)


def _lane_to_rows(v):
    n = v.shape[1]
    return jnp.transpose(jnp.broadcast_to(v, (n, n)))


def _sample_attn_kernel(pt_ref, qt_ref, scol_ref, qpos_ref, lam_ref, subg_ref, knew_ref, vnew_ref, *rest,
                        n_pages, n_new, past_len, lam_init):
    k_refs = rest[:n_pages]
    v_refs = rest[n_pages:2 * n_pages]
    o_ref, mb_ref, m_ref, l_ref, acc_ref = rest[2 * n_pages:]
    g = pl.program_id(1)
    n_cols = 2 * n_new * N_HEADS
    page_rows = PAGE_SIZE * N_HEADS
    scol = scol_ref[...]
    qpos = qpos_ref[...]
    qt = qt_ref[...]

    def head_match(n_rows):
        row = lax.broadcasted_iota(jnp.int32, (n_rows, SA_COLS), 0)
        col = lax.broadcasted_iota(jnp.int32, (n_rows, SA_COLS), 1)
        return row // N_HEADS, col, ((row % N_HEADS) == (col % N_HEADS)) & (col < n_cols)

    @pl.when(g == 0)
    def _():
        tok, _, ok = head_match(page_rows)
        mb_ref[...] = jnp.where(ok, scol * tok.astype(F32), NEG_BIG)
        m_ref[...] = jnp.full(m_ref.shape, NEG_BIG, F32)
        l_ref[...] = jnp.zeros(l_ref.shape, F32)
        acc_ref[...] = jnp.zeros(acc_ref.shape, F32)

    def keys(k_ref, n_rows):
        return jnp.concatenate([k_ref[pl.ds(c, n_rows, stride=2), :] for c in range(2)], axis=1).astype(BF16)

    def update(tiles):
        m_prev = m_ref[...]
        m_next = m_prev
        for s, a, _ in tiles:
            m_next = jnp.maximum(m_next, jnp.max(s, axis=0, keepdims=True) + a)
        alpha = jnp.exp2(m_prev - m_next)
        l_new = alpha * l_ref[...]
        pv = None
        for s, a, v in tiles:
            p = jnp.exp2(s - (m_next - a))
            l_new = l_new + jnp.sum(p, axis=0, keepdims=True)
            d = lax.dot_general(p.astype(BF16), v.astype(BF16), (((0,), (0,)), ((), ())),
                                preferred_element_type=F32)
            pv = d if pv is None else pv + d
        l_ref[...] = l_new
        m_ref[...] = m_next
        a_rows = _lane_to_rows(alpha)
        acc_ref[...] = jnp.concatenate([a_rows] * (V_DIM // SA_COLS), axis=1) * acc_ref[...] + pv

    tiles = []
    for p in range(n_pages):
        base = ((g * n_pages + p) * PAGE_SIZE).astype(F32)
        s = jnp.dot(keys(k_refs[p], page_rows), qt, preferred_element_type=F32) + mb_ref[...]
        tiles.append((s, -scol * (qpos - base), v_refs[p][...]))
    update(tiles)

    @pl.when(g == pl.num_programs(1) - 1)
    def _():
        new_rows = n_new * N_HEADS
        t_key, col, ok = head_match(new_rows)
        ok = ok & (t_key <= (col // N_HEADS) % n_new)
        s = jnp.dot(keys(knew_ref, new_rows), qt, preferred_element_type=F32)
        s = jnp.where(ok, s + scol * t_key.astype(F32), NEG_BIG)
        update([(s, -scol * (qpos - past_len), vnew_ref[...])])

        lam = _lambda_full(lam_ref, lam_init)
        l_rows = _lane_to_rows(l_ref[...])
        pn = acc_ref[...] / jnp.concatenate([l_rows] * (V_DIM // SA_COLS), axis=1)
        d = pn[:new_rows] - lam * pn[new_rows:2 * new_rows]
        o_ref[...] = _rms(d, subg_ref[...]) * (1.0 - lam_init)


def sample_attention(qt, scol, qpos, lam_vecs, subg, knew, vnew, cache_k2d, cache_v2d, page_table,
                     lam_init, n_new, n_pages):
    nb = qt.shape[0]
    pages_per_seq = page_table.shape[1]
    past_len = pages_per_seq * PAGE_SIZE
    new_rows = n_new * N_HEADS

    def page_spec(p, rows, width):
        return pl.BlockSpec((rows, width), lambda b, g, pt: (pt[b, g * n_pages + p], 0))

    const2 = lambda b, g, pt: (0, 0)
    per_b = lambda b, g, pt: (b, 0, 0)
    grid_spec = pltpu.PrefetchScalarGridSpec(
        num_scalar_prefetch=1,
        grid=(nb, pages_per_seq // n_pages),
        in_specs=[pl.BlockSpec((None, 2 * HEAD_DIM, SA_COLS), per_b),
                  pl.BlockSpec((1, SA_COLS), const2),
                  pl.BlockSpec((1, SA_COLS), const2),
                  pl.BlockSpec((4, HEAD_DIM), const2),
                  pl.BlockSpec((1, V_DIM), const2),
                  pl.BlockSpec((None, 2 * new_rows, HEAD_DIM), per_b),
                  pl.BlockSpec((None, new_rows, V_DIM), per_b)]
                 + [page_spec(p, 2 * PAGE_SIZE * N_HEADS, HEAD_DIM) for p in range(n_pages)]
                 + [page_spec(p, PAGE_SIZE * N_HEADS, V_DIM) for p in range(n_pages)],
        out_specs=pl.BlockSpec((None, new_rows, V_DIM), per_b),
        scratch_shapes=[pltpu.VMEM((PAGE_SIZE * N_HEADS, SA_COLS), F32),
                        pltpu.VMEM((1, SA_COLS), F32), pltpu.VMEM((1, SA_COLS), F32),
                        pltpu.VMEM((SA_COLS, V_DIM), F32)],
    )
    return pl.pallas_call(
        functools.partial(_sample_attn_kernel, n_pages=n_pages, n_new=n_new, past_len=past_len,
                          lam_init=lam_init),
        grid_spec=grid_spec,
        out_shape=jax.ShapeDtypeStruct((nb, new_rows, V_DIM), F32),
        compiler_params=_params("parallel", "arbitrary"),
        name="sample_attn",
    )(page_table, qt, scol, qpos, lam_vecs, subg, knew, vnew,
      *([cache_k2d] * n_pages), *([cache_v2d] * n_pages))


def _ln_swish(y, g, b):
    mu = jnp.mean(y, axis=-1, keepdims=True)
    yc = y - mu
    var = jnp.mean(yc * yc, axis=-1, keepdims=True)
    z = yc * lax.rsqrt(var + LN_EPS) * g + b
    return z * jax.nn.sigmoid(z)


def _prompt_conv_kernel(hist_ref, halo_ref, cur_ref, cw_ref, cb_ref, lg_ref, lb_ref, o_ref, buf_ref, y_ref,
                        *, tt, rb, lanes):
    i = pl.program_id(0)
    use_prev = jnp.broadcast_to(i > 0, halo_ref.shape)
    buf_ref[0:HALO, :] = jnp.where(use_prev, halo_ref[...], hist_ref[...])
    buf_ref[HALO:HALO + tt, :] = cur_ref[...]
    shift = HALO - (CONV_W - 1)

    def chunk(cc, carry):
        c0 = pl.multiple_of(cc * lanes, lanes)
        cols = pl.ds(c0, lanes)
        for r0 in range(0, tt, rb):
            acc = jnp.broadcast_to(cb_ref[:, cols], (rb, lanes))
            for w in range(CONV_W):
                acc = acc + buf_ref[pl.ds(r0 + w + shift, rb), cols] * cw_ref[pl.ds(w, 1), cols]
            y_ref[pl.ds(r0, rb), cols] = acc
        return carry

    lax.fori_loop(0, CONV_CH // lanes, chunk, 0)
    o_ref[...] = _ln_swish(y_ref[...], lg_ref[...], lb_ref[...]).astype(BF16)


def prompt_conv(glu, hist, cw, cb, lg, lb, tt):
    t, c = glu.shape
    const = lambda i: (0, 0)
    return pl.pallas_call(
        functools.partial(_prompt_conv_kernel, tt=tt, rb=64, lanes=128),
        grid=(t // tt,),
        in_specs=[pl.BlockSpec((HALO, c), const),
                  pl.BlockSpec((HALO, c), lambda i: (jnp.maximum(i * (tt // HALO) - 1, 0), 0)),
                  pl.BlockSpec((tt, c), lambda i: (i, 0)),
                  pl.BlockSpec((CONV_W, c), const),
                  pl.BlockSpec((1, c), const), pl.BlockSpec((1, c), const), pl.BlockSpec((1, c), const)],
        out_specs=pl.BlockSpec((tt, c), lambda i: (i, 0)),
        out_shape=jax.ShapeDtypeStruct((t, c), BF16),
        scratch_shapes=[pltpu.VMEM((HALO + tt, c), F32), pltpu.VMEM((tt, c), F32)],
        compiler_params=_params("arbitrary"),
        name="prompt_conv",
    )(hist, glu, glu, cw, cb, lg, lb)


def _sample_conv_kernel(hist_ref, glu_ref, cw_ref, cb_ref, lg_ref, lb_ref, o_ref, nh_ref, full_ref, *, n_new):
    nh = CONV_W - 1
    full_ref[0:nh, :] = hist_ref[0]
    full_ref[nh:nh + n_new, :] = glu_ref[0]
    acc = jnp.broadcast_to(cb_ref[...], (n_new, cb_ref.shape[1]))
    for w in range(CONV_W):
        acc = acc + full_ref[w:w + n_new, :] * cw_ref[w:w + 1, :]
    o_ref[0] = _ln_swish(acc, lg_ref[...], lb_ref[...])
    nh_ref[0] = full_ref[n_new:n_new + nh, :]


def sample_conv(glu, hist, cw, cb, lg, lb):
    nb, n_new, c = glu.shape
    nh = CONV_W - 1
    const = lambda b: (0, 0)
    per_b = lambda b: (b, 0, 0)
    return pl.pallas_call(
        functools.partial(_sample_conv_kernel, n_new=n_new),
        grid=(nb,),
        in_specs=[pl.BlockSpec((1, nh, c), per_b), pl.BlockSpec((1, n_new, c), per_b),
                  pl.BlockSpec((CONV_W, c), const),
                  pl.BlockSpec((1, c), const), pl.BlockSpec((1, c), const), pl.BlockSpec((1, c), const)],
        out_specs=[pl.BlockSpec((1, n_new, c), per_b), pl.BlockSpec((1, nh, c), per_b)],
        out_shape=[jax.ShapeDtypeStruct((nb, n_new, c), F32), jax.ShapeDtypeStruct((nb, nh, c), F32)],
        scratch_shapes=[pltpu.VMEM((nh + n_new + 6, c), F32)],
        compiler_params=_params("parallel"),
        name="sample_conv",
    )(hist, glu, cw, cb, lg, lb)


def _query_columns(qb, nb, n_new):
    w = n_new * N_HEADS
    q = qb.reshape(nb, n_new, N_HEADS, 2, HEAD_DIM).transpose(0, 3, 4, 1, 2).reshape(nb, 2, HEAD_DIM, w)
    zeros = lambda n: jnp.zeros((nb, HEAD_DIM, n), qb.dtype)
    top = jnp.concatenate([q[:, 0], zeros(SA_COLS - w)], axis=-1)
    bot = jnp.concatenate([zeros(w), q[:, 1], zeros(SA_COLS - 2 * w)], axis=-1)
    return jnp.concatenate([top, bot], axis=1)


def _layer(x_p, x_s, ck, cv, hist_s, page_table, w, lam_init, slopes):
    t_p = x_p.shape[0]
    nb, n_new = hist_s.shape[0], x_s.shape[0] // hist_s.shape[0]
    tm_p, tm_s = 512, x_s.shape[0]
    outs = {}
    for name, x, tm, tm_big in (("p", x_p, tm_p, 1024), ("s", x_s, tm_s, tm_s)):
        h = prenorm(x, w["ffn1_pre"], tm)
        a = gateup(h, w["ffn1_wg"], w["ffn1_wu"], tm_big, 512)
        x, h = rowmm([a], [w["ffn1_wd"]], x, w["ffn1_post"], w["mix_pre"], 0.5, tm, 512)
        qb, k32, v32, kb, vb, glu = in_proj(h, w["w_in"], tm_big, 512)
        if name == "p":
            o = prompt_attention(qb, kb, vb, slopes, w["lam_vecs"], w["subln"], lam_init, 512, 512)
            hist0 = jnp.zeros((HALO, CONV_CH), F32)
            c = prompt_conv(glu, hist0, w["conv_w"], w["conv_b"], w["conv_ln_g"], w["conv_ln_b"], 256)
            new_hist = glu[t_p - (CONV_W - 1):][None]
        else:
            col = jnp.arange(SA_COLS)
            used = col < 2 * n_new * N_HEADS
            scol = jnp.where(used, slopes[col % N_HEADS] * LOG2E, 0.0)[None, :]
            qpos = (page_table.shape[1] * PAGE_SIZE + (col // N_HEADS) % n_new).astype(F32)[None, :]
            o = sample_attention(_query_columns(qb, nb, n_new), scol, qpos, w["lam_vecs"], w["subln"],
                                 k32.reshape(nb, n_new * N_HEADS * 2, HEAD_DIM),
                                 v32.reshape(nb, n_new * N_HEADS, V_DIM),
                                 ck.reshape(-1, HEAD_DIM), cv.reshape(-1, V_DIM),
                                 page_table, lam_init, n_new, 8)
            o = o.reshape(nb * n_new, D_ATTN).astype(BF16)
            c, new_hist = sample_conv(glu.reshape(nb, n_new, CONV_CH), hist_s, w["conv_w"], w["conv_b"],
                                      w["conv_ln_g"], w["conv_ln_b"])
            c = c.reshape(nb * n_new, CONV_CH).astype(BF16)
        x, h = rowmm([o, c], [w["w_out_a"], w["w_out_c"]], x, w["mix_post"], w["ffn2_pre"], 1.0, tm, 256)
        a = gateup(h, w["ffn2_wg"], w["ffn2_wu"], tm_big, 512)
        x, _ = rowmm([a], [w["ffn2_wd"]], x, w["ffn2_post"], None, 0.5, tm, 512)
        outs[name] = (x, k32, v32, new_hist)
    return outs


def kernel(x_prompt, x_sample, cache_k, cache_v, state_conv, page_table, ffn1_pre_g, ffn1_w_gate, ffn1_w_up, ffn1_w_down, ffn1_post_g, mix_pre_g, w_in, lambda_q1, lambda_k1, lambda_q2, lambda_k2, subln_g, conv_w, conv_b, conv_ln_g, conv_ln_b, w_out, mix_post_g, ffn2_pre_g, ffn2_w_gate, ffn2_w_up, ffn2_w_down, ffn2_post_g):
    depth = w_in.shape[0]
    batch, seq, _ = x_prompt.shape
    nb, n_new, _ = x_sample.shape
    slopes = 2.0 ** (-8.0 * jnp.arange(1, N_HEADS + 1, dtype=F32) / N_HEADS)
    ff_pad = D_FF_PAD - D_FF

    def up_w(wt):
        return jnp.pad(wt, ((0, 0), (0, ff_pad))).astype(BF16)

    def down_w(wt):
        return jnp.pad(wt, ((0, ff_pad), (0, 0))).astype(BF16)

    row = lambda v: v.reshape(1, -1).astype(F32)
    assert batch == 1
    x_p = x_prompt.reshape(seq, D_MODEL)
    x_s = x_sample.reshape(nb * n_new, D_MODEL)
    kp, vp, cp, ks, vs, cs = [], [], [], [], [], []
    for l in range(depth):
        wo = w_out[l].astype(BF16)
        w = {
            "ffn1_pre": row(ffn1_pre_g[l]), "ffn1_wg": up_w(ffn1_w_gate[l]), "ffn1_wu": up_w(ffn1_w_up[l]),
            "ffn1_wd": down_w(ffn1_w_down[l]), "ffn1_post": row(ffn1_post_g[l]),
            "mix_pre": row(mix_pre_g[l]), "w_in": w_in[l].astype(BF16), "subln": row(subln_g[l]),
            "lam_vecs": jnp.stack([lambda_q1[l], lambda_k1[l], lambda_q2[l], lambda_k2[l]]).astype(F32),
            "conv_w": conv_w[l].astype(F32), "conv_b": row(conv_b[l]),
            "conv_ln_g": row(conv_ln_g[l]), "conv_ln_b": row(conv_ln_b[l]),
            "w_out_a": wo[:D_ATTN], "w_out_c": wo[D_ATTN:], "mix_post": row(mix_post_g[l]),
            "ffn2_pre": row(ffn2_pre_g[l]), "ffn2_wg": up_w(ffn2_w_gate[l]), "ffn2_wu": up_w(ffn2_w_up[l]),
            "ffn2_wd": down_w(ffn2_w_down[l]), "ffn2_post": row(ffn2_post_g[l]),
        }
        outs = _layer(x_p, x_s, cache_k[l], cache_v[l], state_conv[l], page_table, w, _lambda_init(l), slopes)
        x_p, k32, v32, nh = outs["p"]
        kp.append(k32.reshape(batch, seq, N_HEADS, 2, HEAD_DIM))
        vp.append(v32.reshape(batch, seq, N_HEADS, V_DIM))
        cp.append(nh)
        x_s, k32, v32, nh = outs["s"]
        ks.append(k32.reshape(nb, n_new, N_HEADS, 2, HEAD_DIM))
        vs.append(v32.reshape(nb, n_new, N_HEADS, V_DIM))
        cs.append(nh)
    return (x_p.reshape(batch, seq, D_MODEL), x_s.reshape(nb, n_new, D_MODEL),
            jnp.stack(kp), jnp.stack(vp), jnp.stack(cp), jnp.stack(ks), jnp.stack(vs), jnp.stack(cs))
```

```python
import functools
import math

import jax
import jax.numpy as jnp
from jax import lax
from jax.experimental import pallas as pl
from jax.experimental.pallas import tpu as pltpu

F32 = jnp.float32
BF16 = jnp.bfloat16

D_MODEL = 4096
HEAD_DIM = 128
V_DIM = 2 * HEAD_DIM
N_HEADS = D_MODEL // 512
D_ATTN = N_HEADS * V_DIM
CONV_CH = D_MODEL - D_ATTN
QK_COLS = N_HEADS * 2 * HEAD_DIM
CONV_W = 31
D_FF = 11008
NORM_EPS = 1e-6
LN_EPS = 1e-5
PAGE_SIZE = 128
ATTN_SCALE = HEAD_DIM ** -0.5
LOG2E = math.log2(math.e)
NEG_BIG = -1e30
SA_COLS = 128

FF_TILE = 1024
D_FF_PAD = -(-D_FF // FF_TILE) * FF_TILE
SUBLANES = 8
ROW_CHUNK = 64
EPI_ROWS = 16
HALO = 32
V7X_VMEM_LIMIT = 56 * 1024 * 1024


def _lambda_init(layer):
    return 0.8 - 0.6 * math.exp(-0.3 * layer)


def _params(*sem):
    return pltpu.CompilerParams(dimension_semantics=sem, vmem_limit_bytes=V7X_VMEM_LIMIT)


def _rms(x, g):
    ms = jnp.mean(x * x, axis=-1, keepdims=True)
    return x * lax.rsqrt(ms + NORM_EPS) * g


def _cast_pad_cols_kernel(w_ref, o_ref):
    n = w_ref.shape[1]
    o_ref[:, :n] = w_ref[...].astype(BF16)
    o_ref[:, n:] = jnp.zeros((o_ref.shape[0], o_ref.shape[1] - n), BF16)


def cast_pad_cols(w, n_pad, tr=256):
    r, n = w.shape
    return pl.pallas_call(
        _cast_pad_cols_kernel,
        grid=(r // tr,),
        in_specs=[pl.BlockSpec((tr, n), lambda i: (i, 0))],
        out_specs=pl.BlockSpec((tr, n_pad), lambda i: (i, 0)),
        out_shape=jax.ShapeDtypeStruct((r, n_pad), BF16),
        compiler_params=_params("parallel"),
        name="cast_pad_cols",
    )(w)


def _cast_pad_rows_kernel(w_ref, o_ref, *, n_valid):
    i = pl.program_id(0)

    @pl.when(i < n_valid)
    def _():
        o_ref[...] = w_ref[...].astype(BF16)

    @pl.when(i >= n_valid)
    def _():
        o_ref[...] = jnp.zeros(o_ref.shape, BF16)


def cast_pad_rows(w, r_pad, n_blocks=16):
    r, n = w.shape
    tr = r // n_blocks
    assert tr * n_blocks == r and tr % 16 == 0 and r_pad - r <= tr
    return pl.pallas_call(
        functools.partial(_cast_pad_rows_kernel, n_valid=n_blocks),
        grid=(n_blocks + 1,),
        in_specs=[pl.BlockSpec((tr, n), lambda i: (jnp.minimum(i, n_blocks - 1), 0))],
        out_specs=pl.BlockSpec((tr, n), lambda i: (i, 0)),
        out_shape=jax.ShapeDtypeStruct((r_pad, n), BF16),
        compiler_params=_params("parallel"),
        name="cast_pad_rows",
    )(w)


def _prenorm_kernel(x_ref, g_ref, h_ref):
    h_ref[...] = _rms(x_ref[...], g_ref[...]).astype(BF16)


def prenorm(x, g, tm):
    m, d = x.shape
    return pl.pallas_call(
        _prenorm_kernel,
        grid=(m // tm,),
        in_specs=[pl.BlockSpec((tm, d), lambda i: (i, 0)),
                  pl.BlockSpec((1, d), lambda i: (0, 0))],
        out_specs=pl.BlockSpec((tm, d), lambda i: (i, 0)),
        out_shape=jax.ShapeDtypeStruct((m, d), BF16),
        compiler_params=_params("parallel"),
        name="prenorm",
    )(x, g)


def _gateup_kernel(h_ref, wg_ref, wu_ref, a_ref):
    h = h_ref[...]
    g = jnp.dot(h, wg_ref[...], preferred_element_type=F32)
    u = jnp.dot(h, wu_ref[...], preferred_element_type=F32)
    a_ref[...] = (g * jax.nn.sigmoid(g) * u).astype(BF16)


def gateup(h, wg, wu, tm, tn):
    m, d = h.shape
    f = wg.shape[1]
    return pl.pallas_call(
        _gateup_kernel,
        grid=(m // tm, f // tn),
        in_specs=[pl.BlockSpec((tm, d), lambda i, j: (i, 0)),
                  pl.BlockSpec((d, tn), lambda i, j: (0, j)),
                  pl.BlockSpec((d, tn), lambda i, j: (0, j))],
        out_specs=pl.BlockSpec((tm, tn), lambda i, j: (i, j)),
        out_shape=jax.ShapeDtypeStruct((m, f), BF16),
        compiler_params=_params("parallel", "arbitrary"),
        name="gateup",
    )(h, wg, wu)


def _rowmm_kernel(*refs, n_a, res_scale, has_next):
    a_refs = refs[:n_a]
    w_refs = refs[n_a:2 * n_a]
    x_ref, gpost_ref = refs[2 * n_a], refs[2 * n_a + 1]
    if has_next:
        gnext_ref, xo_ref, ho_ref = refs[2 * n_a + 2:]
    else:
        (xo_ref,) = refs[2 * n_a + 2:]
    k = pl.program_id(1)

    @pl.when(k == 0)
    def _():
        xo_ref[...] = jnp.zeros(xo_ref.shape, F32)

    for a_ref, w_ref in zip(a_refs, w_refs):
        xo_ref[...] += jnp.dot(a_ref[...], w_ref[...], preferred_element_type=F32)

    @pl.when(k == pl.num_programs(1) - 1)
    def _():
        def rows(r, carry):
            sl = pl.ds(pl.multiple_of(r * EPI_ROWS, EPI_ROWS), EPI_ROWS)
            xn = x_ref[sl, :] + res_scale * _rms(xo_ref[sl, :], gpost_ref[...])
            xo_ref[sl, :] = xn
            if has_next:
                ho_ref[sl, :] = _rms(xn, gnext_ref[...]).astype(BF16)
            return carry

        lax.fori_loop(0, xo_ref.shape[0] // EPI_ROWS, rows, 0, unroll=2)


def rowmm(a_list, w_list, x, g_post, g_next, res_scale, tm, tk, w_row_offsets=None):
    m, d = x.shape
    kdim = a_list[0].shape[1]
    n_a = len(a_list)
    has_next = g_next is not None
    w_row_offsets = w_row_offsets or (0,) * n_a

    def w_spec(row_offset):
        return pl.BlockSpec((tk, d), lambda i, k: (k + row_offset // tk, 0))

    in_specs = ([pl.BlockSpec((tm, tk), lambda i, k: (i, k)) for _ in a_list]
                + [w_spec(off) for off in w_row_offsets]
                + [pl.BlockSpec((tm, d), lambda i, k: (i, 0), pipeline_mode=pl.Buffered(1)),
                   pl.BlockSpec((1, d), lambda i, k: (0, 0))])
    args = list(a_list) + list(w_list) + [x, g_post]
    out_specs = [pl.BlockSpec((tm, d), lambda i, k: (i, 0))]
    out_shape = [jax.ShapeDtypeStruct((m, d), F32)]
    if has_next:
        in_specs.append(pl.BlockSpec((1, d), lambda i, k: (0, 0)))
        args.append(g_next)
        out_specs.append(pl.BlockSpec((tm, d), lambda i, k: (i, 0)))
        out_shape.append(jax.ShapeDtypeStruct((m, d), BF16))
    outs = pl.pallas_call(
        functools.partial(_rowmm_kernel, n_a=n_a, res_scale=res_scale, has_next=has_next),
        grid=(m // tm, kdim // tk),
        in_specs=in_specs,
        out_specs=out_specs,
        out_shape=out_shape,
        compiler_params=_params("parallel", "arbitrary"),
        name="rowmm",
    )(*args)
    return outs if has_next else (outs[0], None)


def _proj_q_kernel(h_ref, w_ref, q_ref):
    q = jnp.dot(h_ref[...], w_ref[...], preferred_element_type=F32)
    q_ref[...] = (q * (ATTN_SCALE * LOG2E)).astype(BF16)


def _proj_kv_kernel(h_ref, wk_ref, wv_ref, k32_ref, v32_ref, kb_ref, vb_ref):
    h = h_ref[...]
    k = jnp.dot(h, wk_ref[...], preferred_element_type=F32)
    v = jnp.dot(h, wv_ref[...], preferred_element_type=F32)
    k32_ref[...] = k
    v32_ref[...] = v
    kb_ref[...] = k.astype(BF16)
    vb_ref[...] = v.astype(BF16)


def _proj_glu_kernel(h_ref, wa_ref, wg_ref, o_ref):
    h = h_ref[...]
    a = jnp.dot(h, wa_ref[...], preferred_element_type=F32)
    g = jnp.dot(h, wg_ref[...], preferred_element_type=F32)
    o_ref[...] = a * jax.nn.sigmoid(g)


def in_proj(h, w_in, tm, tn):
    m, d = h.shape
    nq = QK_COLS // tn
    nv = D_ATTN // tn
    nc = CONV_CH // tn
    h_spec = pl.BlockSpec((tm, d), lambda i, j: (i, 0))
    o_spec = pl.BlockSpec((tm, tn), lambda i, j: (i, j))

    def w_spec(off):
        return pl.BlockSpec((d, tn), lambda i, j: (0, j + off))

    qb = pl.pallas_call(
        _proj_q_kernel, grid=(m // tm, nq),
        in_specs=[h_spec, w_spec(0)], out_specs=o_spec,
        out_shape=jax.ShapeDtypeStruct((m, QK_COLS), BF16),
        compiler_params=_params("parallel", "arbitrary"), name="proj_q",
    )(h, w_in)
    k32, v32, kb, vb = pl.pallas_call(
        _proj_kv_kernel, grid=(m // tm, nv),
        in_specs=[h_spec, w_spec(nq), w_spec(2 * nq)], out_specs=[o_spec] * 4,
        out_shape=[jax.ShapeDtypeStruct((m, QK_COLS), F32), jax.ShapeDtypeStruct((m, D_ATTN), F32),
                   jax.ShapeDtypeStruct((m, QK_COLS), BF16), jax.ShapeDtypeStruct((m, D_ATTN), BF16)],
        compiler_params=_params("parallel", "arbitrary"), name="proj_kv",
    )(h, w_in, w_in)
    glu = pl.pallas_call(
        _proj_glu_kernel, grid=(m // tm, nc),
        in_specs=[h_spec, w_spec(2 * nq + nv), w_spec(2 * nq + nv + nc)], out_specs=o_spec,
        out_shape=jax.ShapeDtypeStruct((m, CONV_CH), F32),
        compiler_params=_params("parallel", "arbitrary"), name="proj_glu",
    )(h, w_in, w_in)
    return qb, k32, v32, kb, vb, glu


def _lambda_full(lam_ref, lam_init):
    lv = lam_ref[...]
    e1 = jnp.exp(jnp.sum(lv[0:1] * lv[1:2], axis=-1, keepdims=True))
    e2 = jnp.exp(jnp.sum(lv[2:3] * lv[3:4], axis=-1, keepdims=True))
    return e1 - e2 + lam_init


def _prompt_attn_kernel(slopes_ref, lam_ref, subg_ref, q_ref, k_ref, v_ref, o_ref,
                        b0_ref, s_ref, p_ref, alpha_ref, m_ref, l_ref, acc_ref, *, tq, tk, lam_init):
    h = pl.program_id(0)
    qi = pl.program_id(1)
    slope2 = slopes_ref[h] * LOG2E
    nt = (((1,), (1,)), ((), ()))
    per_q = tq // tk

    def rel_local(n_rows):
        return lax.broadcasted_iota(jnp.int32, (n_rows, tk), 0) - lax.broadcasted_iota(jnp.int32, (n_rows, tk), 1)

    b0_ref[...] = -slope2 * rel_local(tq).astype(F32)
    m_ref[...] = jnp.full(m_ref.shape, NEG_BIG, F32)
    l_ref[...] = jnp.zeros(l_ref.shape, F32)
    acc_ref[...] = jnp.zeros(acc_ref.shape, F32)

    def scores(kj):
        kb = k_ref[pl.ds(pl.multiple_of(kj * tk, tk), tk), :]
        for c in range(2):
            cols = slice(c * HEAD_DIM, (c + 1) * HEAD_DIM)
            s_ref[kj % 2, c] = lax.dot_general(q_ref[:, cols], kb[:, cols], nt, preferred_element_type=F32)

    def consume(kj, diag_offset):
        vb = v_ref[pl.ds(pl.multiple_of(kj * tk, tk), tk), :]
        c_blk = -slope2 * (qi * tq - kj * tk).astype(F32)
        for c in range(2):
            for r in range(0, tq, ROW_CHUNK):
                rows = slice(r, r + ROW_CHUNK)
                s = s_ref[kj % 2, c, rows, :] + b0_ref[rows, :]
                if diag_offset is not None:
                    s = jnp.where(rel_local(ROW_CHUNK) >= diag_offset - r, s, NEG_BIG)
                m_prev = m_ref[c, rows, :]
                m_next = jnp.maximum(m_prev, jnp.max(s, axis=1, keepdims=True) + c_blk)
                alpha = jnp.exp2(m_prev - m_next)
                p = jnp.exp2(s - pltpu.repeat(m_next - c_blk, tk // 128, axis=1))
                l_ref[c, rows, :] = alpha * l_ref[c, rows, :] + jnp.sum(p, axis=1, keepdims=True)
                m_ref[c, rows, :] = m_next
                alpha_ref[c, rows, :] = alpha
                p_ref[c, rows, :] = p.astype(BF16)
            acc_ref[c] = (pltpu.repeat(alpha_ref[c], V_DIM // 128, axis=1) * acc_ref[c]
                          + jnp.dot(p_ref[c], vb, preferred_element_type=F32))

    n_full = qi * per_q
    scores(0)

    def body(kj, carry):
        consume(kj, None)
        scores(kj + 1)
        return carry

    lax.fori_loop(0, n_full, body, 0)
    for d in range(per_q):
        if d + 1 < per_q:
            scores(n_full + d + 1)
        consume(n_full + d, d * tk)

    lam = _lambda_full(lam_ref, lam_init)
    o = (acc_ref[0] / pltpu.repeat(l_ref[0], V_DIM // 128, axis=1)
         - lam * (acc_ref[1] / pltpu.repeat(l_ref[1], V_DIM // 128, axis=1)))
    o_ref[...] = (_rms(o, subg_ref[...]) * (1.0 - lam_init)).astype(BF16)


def prompt_attention(qb, kb, vb, slopes, lam_vecs, subg, lam_init, tq, tk):
    t = qb.shape[0]
    return pl.pallas_call(
        functools.partial(_prompt_attn_kernel, tq=tq, tk=tk, lam_init=lam_init),
        grid=(N_HEADS, t // tq),
        in_specs=[pl.BlockSpec(memory_space=pltpu.SMEM),
                  pl.BlockSpec((4, HEAD_DIM), lambda h, i: (0, 0)),
                  pl.BlockSpec((1, V_DIM), lambda h, i: (0, 0)),
                  pl.BlockSpec((tq, V_DIM), lambda h, i: (i, h)),
                  pl.BlockSpec((t, V_DIM), lambda h, i: (0, h)),
                  pl.BlockSpec((t, V_DIM), lambda h, i: (0, h))],
        out_specs=pl.BlockSpec((tq, V_DIM), lambda h, i: (i, h)),
        out_shape=jax.ShapeDtypeStruct((t, D_ATTN), BF16),
        scratch_shapes=[pltpu.VMEM((tq, tk), F32), pltpu.VMEM((2, 2, tq, tk), F32),
                        pltpu.VMEM((2, tq, tk), BF16), pltpu.VMEM((2, tq, 128), F32),
                        pltpu.VMEM((2, tq, 128), F32), pltpu.VMEM((2, tq, 128), F32),
                        pltpu.VMEM((2, tq, V_DIM), F32)],
        compiler_params=_params("parallel", "arbitrary"),
        name="prompt_attn",
    )(slopes, lam_vecs, subg, qb, kb, vb)


def _lane_to_rows(v):
    n = v.shape[1]
    return jnp.transpose(jnp.broadcast_to(v, (n, n)))


def _sample_attn_kernel(pt_ref, qt_ref, scol_ref, qpos_ref, lam_ref, subg_ref, knew_ref, vnew_ref, *rest,
                        n_pages, n_new, past_len, lam_init):
    k_refs = rest[:n_pages]
    v_refs = rest[n_pages:2 * n_pages]
    o_ref, mb_ref, m_ref, l_ref, acc_ref = rest[2 * n_pages:]
    g = pl.program_id(1)
    n_cols = 2 * n_new * N_HEADS
    page_rows = PAGE_SIZE * N_HEADS
    scol = scol_ref[...]
    qpos = qpos_ref[...]
    qt = qt_ref[...]

    def head_match(n_rows):
        row = lax.broadcasted_iota(jnp.int32, (n_rows, SA_COLS), 0)
        col = lax.broadcasted_iota(jnp.int32, (n_rows, SA_COLS), 1)
        return row // N_HEADS, col, ((row % N_HEADS) == (col % N_HEADS)) & (col < n_cols)

    @pl.when(g == 0)
    def _():
        tok, _, ok = head_match(page_rows)
        mb_ref[...] = jnp.where(ok, scol * tok.astype(F32), NEG_BIG)
        m_ref[...] = jnp.full(m_ref.shape, NEG_BIG, F32)
        l_ref[...] = jnp.zeros(l_ref.shape, F32)
        acc_ref[...] = jnp.zeros(acc_ref.shape, F32)

    def keys(k_ref, n_rows):
        return jnp.concatenate([k_ref[pl.ds(c, n_rows, stride=2), :] for c in range(2)], axis=1).astype(BF16)

    def update(tiles):
        m_prev = m_ref[...]
        m_next = m_prev
        for s, a, _ in tiles:
            m_next = jnp.maximum(m_next, jnp.max(s, axis=0, keepdims=True) + a)
        alpha = jnp.exp2(m_prev - m_next)
        l_new = alpha * l_ref[...]
        pv = None
        for s, a, v in tiles:
            p = jnp.exp2(s - (m_next - a))
            l_new = l_new + jnp.sum(p, axis=0, keepdims=True)
            d = lax.dot_general(p.astype(BF16), v.astype(BF16), (((0,), (0,)), ((), ())),
                                preferred_element_type=F32)
            pv = d if pv is None else pv + d
        l_ref[...] = l_new
        m_ref[...] = m_next
        a_rows = _lane_to_rows(alpha)
        acc_ref[...] = jnp.concatenate([a_rows] * (V_DIM // SA_COLS), axis=1) * acc_ref[...] + pv

    tiles = []
    for p in range(n_pages):
        base = ((g * n_pages + p) * PAGE_SIZE).astype(F32)
        s = jnp.dot(keys(k_refs[p], page_rows), qt, preferred_element_type=F32) + mb_ref[...]
        tiles.append((s, -scol * (qpos - base), v_refs[p][...]))
    update(tiles)

    @pl.when(g == pl.num_programs(1) - 1)
    def _():
        new_rows = n_new * N_HEADS
        t_key, col, ok = head_match(new_rows)
        ok = ok & (t_key <= (col // N_HEADS) % n_new)
        s = jnp.dot(keys(knew_ref, new_rows), qt, preferred_element_type=F32)
        s = jnp.where(ok, s + scol * t_key.astype(F32), NEG_BIG)
        update([(s, -scol * (qpos - past_len), vnew_ref[...])])

        lam = _lambda_full(lam_ref, lam_init)
        l_rows = _lane_to_rows(l_ref[...])
        pn = acc_ref[...] / jnp.concatenate([l_rows] * (V_DIM // SA_COLS), axis=1)
        d = pn[:new_rows] - lam * pn[new_rows:2 * new_rows]
        o_ref[...] = _rms(d, subg_ref[...]) * (1.0 - lam_init)


def sample_attention(qt, scol, qpos, lam_vecs, subg, knew, vnew, cache_k2d, cache_v2d, page_table,
                     lam_init, n_new, n_pages):
    nb = qt.shape[0]
    pages_per_seq = page_table.shape[1]
    past_len = pages_per_seq * PAGE_SIZE
    new_rows = n_new * N_HEADS

    def page_spec(p, rows, width):
        return pl.BlockSpec((rows, width), lambda b, g, pt: (pt[b, g * n_pages + p], 0))

    const2 = lambda b, g, pt: (0, 0)
    per_b = lambda b, g, pt: (b, 0, 0)
    grid_spec = pltpu.PrefetchScalarGridSpec(
        num_scalar_prefetch=1,
        grid=(nb, pages_per_seq // n_pages),
        in_specs=[pl.BlockSpec((None, 2 * HEAD_DIM, SA_COLS), per_b),
                  pl.BlockSpec((1, SA_COLS), const2),
                  pl.BlockSpec((1, SA_COLS), const2),
                  pl.BlockSpec((4, HEAD_DIM), const2),
                  pl.BlockSpec((1, V_DIM), const2),
                  pl.BlockSpec((None, 2 * new_rows, HEAD_DIM), per_b),
                  pl.BlockSpec((None, new_rows, V_DIM), per_b)]
                 + [page_spec(p, 2 * PAGE_SIZE * N_HEADS, HEAD_DIM) for p in range(n_pages)]
                 + [page_spec(p, PAGE_SIZE * N_HEADS, V_DIM) for p in range(n_pages)],
        out_specs=pl.BlockSpec((None, new_rows, V_DIM), per_b),
        scratch_shapes=[pltpu.VMEM((PAGE_SIZE * N_HEADS, SA_COLS), F32),
                        pltpu.VMEM((1, SA_COLS), F32), pltpu.VMEM((1, SA_COLS), F32),
                        pltpu.VMEM((SA_COLS, V_DIM), F32)],
    )
    return pl.pallas_call(
        functools.partial(_sample_attn_kernel, n_pages=n_pages, n_new=n_new, past_len=past_len,
                          lam_init=lam_init),
        grid_spec=grid_spec,
        out_shape=jax.ShapeDtypeStruct((nb, new_rows, V_DIM), F32),
        compiler_params=_params("parallel", "arbitrary"),
        name="sample_attn",
    )(page_table, qt, scol, qpos, lam_vecs, subg, knew, vnew,
      *([cache_k2d] * n_pages), *([cache_v2d] * n_pages))


def _ln_swish(y, g, b):
    mu = jnp.mean(y, axis=-1, keepdims=True)
    yc = y - mu
    var = jnp.mean(yc * yc, axis=-1, keepdims=True)
    z = yc * lax.rsqrt(var + LN_EPS) * g + b
    return z * jax.nn.sigmoid(z)


def _prompt_conv_kernel(hist_ref, halo_ref, cur_ref, cw_ref, cb_ref, lg_ref, lb_ref, o_ref, buf_ref, y_ref,
                        *, tt, rb, lanes):
    i = pl.program_id(0)
    use_prev = jnp.broadcast_to(i > 0, halo_ref.shape)
    buf_ref[0:HALO, :] = jnp.where(use_prev, halo_ref[...], hist_ref[...])
    buf_ref[HALO:HALO + tt, :] = cur_ref[...]
    buf_ref[HALO + tt:, :] = jnp.zeros((SUBLANES, buf_ref.shape[1]), F32)
    shift = HALO - (CONV_W - 1)

    def chunk(cc, carry):
        c0 = pl.multiple_of(cc * lanes, lanes)
        cols = pl.ds(c0, lanes)
        for r0 in range(0, tt, rb):
            acc = jnp.broadcast_to(cb_ref[:, cols], (rb, lanes))
            for delta in range(SUBLANES):
                part = None
                for w in range(CONV_W):
                    if (w + shift) % SUBLANES != delta:
                        continue
                    term = (buf_ref[pl.ds(r0 + w + shift - delta, rb + SUBLANES), cols]
                            * cw_ref[pl.ds(w, 1), cols])
                    part = term if part is None else part + term
                if part is not None:
                    acc = acc + part[delta:delta + rb]
            y_ref[pl.ds(r0, rb), cols] = acc
        return carry

    lax.fori_loop(0, CONV_CH // lanes, chunk, 0)
    o_ref[...] = _ln_swish(y_ref[...], lg_ref[...], lb_ref[...]).astype(BF16)


def prompt_conv(glu, hist, cw, cb, lg, lb, tt):
    t, c = glu.shape
    const = lambda i: (0, 0)
    return pl.pallas_call(
        functools.partial(_prompt_conv_kernel, tt=tt, rb=64, lanes=128),
        grid=(t // tt,),
        in_specs=[pl.BlockSpec((HALO, c), const),
                  pl.BlockSpec((HALO, c), lambda i: (jnp.maximum(i * (tt // HALO) - 1, 0), 0)),
                  pl.BlockSpec((tt, c), lambda i: (i, 0)),
                  pl.BlockSpec((CONV_W, c), const),
                  pl.BlockSpec((1, c), const), pl.BlockSpec((1, c), const), pl.BlockSpec((1, c), const)],
        out_specs=pl.BlockSpec((tt, c), lambda i: (i, 0)),
        out_shape=jax.ShapeDtypeStruct((t, c), BF16),
        scratch_shapes=[pltpu.VMEM((HALO + tt + SUBLANES, c), F32), pltpu.VMEM((tt, c), F32)],
        compiler_params=_params("arbitrary"),
        name="prompt_conv",
    )(hist, glu, glu, cw, cb, lg, lb)


def _sample_conv_kernel(hist_ref, glu_ref, cw_ref, cb_ref, lg_ref, lb_ref, o_ref, nh_ref, full_ref, *, n_new):
    nh = CONV_W - 1
    full_ref[0:nh, :] = hist_ref[0]
    full_ref[nh:nh + n_new, :] = glu_ref[0]
    acc = jnp.broadcast_to(cb_ref[...], (n_new, cb_ref.shape[1]))
    for w in range(CONV_W):
        acc = acc + full_ref[w:w + n_new, :] * cw_ref[w:w + 1, :]
    o_ref[0] = _ln_swish(acc, lg_ref[...], lb_ref[...])
    nh_ref[0] = full_ref[n_new:n_new + nh, :]


def sample_conv(glu, hist, cw, cb, lg, lb):
    nb, n_new, c = glu.shape
    nh = CONV_W - 1
    const = lambda b: (0, 0)
    per_b = lambda b: (b, 0, 0)
    return pl.pallas_call(
        functools.partial(_sample_conv_kernel, n_new=n_new),
        grid=(nb,),
        in_specs=[pl.BlockSpec((1, nh, c), per_b), pl.BlockSpec((1, n_new, c), per_b),
                  pl.BlockSpec((CONV_W, c), const),
                  pl.BlockSpec((1, c), const), pl.BlockSpec((1, c), const), pl.BlockSpec((1, c), const)],
        out_specs=[pl.BlockSpec((1, n_new, c), per_b), pl.BlockSpec((1, nh, c), per_b)],
        out_shape=[jax.ShapeDtypeStruct((nb, n_new, c), F32), jax.ShapeDtypeStruct((nb, nh, c), F32)],
        scratch_shapes=[pltpu.VMEM((nh + n_new + 6, c), F32)],
        compiler_params=_params("parallel"),
        name="sample_conv",
    )(hist, glu, cw, cb, lg, lb)


def _query_columns(qb, nb, n_new):
    w = n_new * N_HEADS
    q = qb.reshape(nb, n_new, N_HEADS, 2, HEAD_DIM).transpose(0, 3, 4, 1, 2).reshape(nb, 2, HEAD_DIM, w)
    zeros = lambda n: jnp.zeros((nb, HEAD_DIM, n), qb.dtype)
    top = jnp.concatenate([q[:, 0], zeros(SA_COLS - w)], axis=-1)
    bot = jnp.concatenate([zeros(w), q[:, 1], zeros(SA_COLS - 2 * w)], axis=-1)
    return jnp.concatenate([top, bot], axis=1)


def _layer(x_p, x_s, ck, cv, hist_s, page_table, w, lam_init, slopes):
    t_p = x_p.shape[0]
    nb, n_new = hist_s.shape[0], x_s.shape[0] // hist_s.shape[0]
    tm_p, tm_s = 512, x_s.shape[0]
    outs = {}
    for name, x, tm, tm_big in (("p", x_p, tm_p, 1024), ("s", x_s, tm_s, tm_s)):
        h = prenorm(x, w["ffn1_pre"], tm)
        a = gateup(h, w["ffn1_wg"], w["ffn1_wu"], tm_big, 512)
        x, h = rowmm([a], [w["ffn1_wd"]], x, w["ffn1_post"], w["mix_pre"], 0.5, tm, 1024)
        qb, k32, v32, kb, vb, glu = in_proj(h, w["w_in"], tm_big, 512)
        if name == "p":
            o = prompt_attention(qb, kb, vb, slopes, w["lam_vecs"], w["subln"], lam_init, 512, 512)
            hist0 = jnp.zeros((HALO, CONV_CH), F32)
            c = prompt_conv(glu, hist0, w["conv_w"], w["conv_b"], w["conv_ln_g"], w["conv_ln_b"], 256)
            new_hist = glu[t_p - (CONV_W - 1):][None]
        else:
            col = jnp.arange(SA_COLS)
            used = col < 2 * n_new * N_HEADS
            scol = jnp.where(used, slopes[col % N_HEADS] * LOG2E, 0.0)[None, :]
            qpos = (page_table.shape[1] * PAGE_SIZE + (col // N_HEADS) % n_new).astype(F32)[None, :]
            o = sample_attention(_query_columns(qb, nb, n_new), scol, qpos, w["lam_vecs"], w["subln"],
                                 k32.reshape(nb, n_new * N_HEADS * 2, HEAD_DIM),
                                 v32.reshape(nb, n_new * N_HEADS, V_DIM),
                                 ck.reshape(-1, HEAD_DIM), cv.reshape(-1, V_DIM),
                                 page_table, lam_init, n_new, 8)
            o = o.reshape(nb * n_new, D_ATTN).astype(BF16)
            c, new_hist = sample_conv(glu.reshape(nb, n_new, CONV_CH), hist_s, w["conv_w"], w["conv_b"],
                                      w["conv_ln_g"], w["conv_ln_b"])
            c = c.reshape(nb * n_new, CONV_CH).astype(BF16)
        x, h = rowmm([o, c], [w["w_out"], w["w_out"]], x, w["mix_post"], w["ffn2_pre"], 1.0, tm, 512,
                     w_row_offsets=(0, D_ATTN))
        a = gateup(h, w["ffn2_wg"], w["ffn2_wu"], tm_big, 512)
        x, _ = rowmm([a], [w["ffn2_wd"]], x, w["ffn2_post"], None, 0.5, tm, 1024)
        outs[name] = (x, k32, v32, new_hist)
    return outs


def kernel(x_prompt, x_sample, cache_k, cache_v, state_conv, page_table, ffn1_pre_g, ffn1_w_gate, ffn1_w_up, ffn1_w_down, ffn1_post_g, mix_pre_g, w_in, lambda_q1, lambda_k1, lambda_q2, lambda_k2, subln_g, conv_w, conv_b, conv_ln_g, conv_ln_b, w_out, mix_post_g, ffn2_pre_g, ffn2_w_gate, ffn2_w_up, ffn2_w_down, ffn2_post_g):
    depth = w_in.shape[0]
    batch, seq, _ = x_prompt.shape
    nb, n_new, _ = x_sample.shape
    slopes = 2.0 ** (-8.0 * jnp.arange(1, N_HEADS + 1, dtype=F32) / N_HEADS)
    up_w = functools.partial(cast_pad_cols, n_pad=D_FF_PAD)
    down_w = functools.partial(cast_pad_rows, r_pad=D_FF_PAD)
    row = lambda v: v.reshape(1, -1).astype(F32)
    assert batch == 1
    x_p = x_prompt.reshape(seq, D_MODEL)
    x_s = x_sample.reshape(nb * n_new, D_MODEL)
    kp, vp, cp, ks, vs, cs = [], [], [], [], [], []
    for l in range(depth):
        wo = w_out[l].astype(BF16)
        w = {
            "ffn1_pre": row(ffn1_pre_g[l]), "ffn1_wg": up_w(ffn1_w_gate[l]), "ffn1_wu": up_w(ffn1_w_up[l]),
            "ffn1_wd": down_w(ffn1_w_down[l]), "ffn1_post": row(ffn1_post_g[l]),
            "mix_pre": row(mix_pre_g[l]), "w_in": w_in[l].astype(BF16), "subln": row(subln_g[l]),
            "lam_vecs": jnp.stack([lambda_q1[l], lambda_k1[l], lambda_q2[l], lambda_k2[l]]).astype(F32),
            "conv_w": conv_w[l].astype(F32), "conv_b": row(conv_b[l]),
            "conv_ln_g": row(conv_ln_g[l]), "conv_ln_b": row(conv_ln_b[l]),
            "w_out": wo, "mix_post": row(mix_post_g[l]),
            "ffn2_pre": row(ffn2_pre_g[l]), "ffn2_wg": up_w(ffn2_w_gate[l]), "ffn2_wu": up_w(ffn2_w_up[l]),
            "ffn2_wd": down_w(ffn2_w_down[l]), "ffn2_post": row(ffn2_post_g[l]),
        }
        outs = _layer(x_p, x_s, cache_k[l], cache_v[l], state_conv[l], page_table, w, _lambda_init(l), slopes)
        x_p, k32, v32, nh = outs["p"]
        kp.append(k32.reshape(batch, seq, N_HEADS, 2, HEAD_DIM))
        vp.append(v32.reshape(batch, seq, N_HEADS, V_DIM))
        cp.append(nh)
        x_s, k32, v32, nh = outs["s"]
        ks.append(k32.reshape(nb, n_new, N_HEADS, 2, HEAD_DIM))
        vs.append(v32.reshape(nb, n_new, N_HEADS, V_DIM))
        cs.append(nh)
    return (x_p.reshape(batch, seq, D_MODEL), x_s.reshape(nb, n_new, D_MODEL),
            jnp.stack(kp), jnp.stack(vp), jnp.stack(cp), jnp.stack(ks), jnp.stack(vs), jnp.stack(cs))
```

```python
import functools
import math

import jax
import jax.numpy as jnp
from jax import lax
from jax.experimental import pallas as pl
from jax.experimental.pallas import tpu as pltpu

F32 = jnp.float32
BF16 = jnp.bfloat16

D_MODEL = 4096
HEAD_DIM = 128
V_DIM = 2 * HEAD_DIM
N_HEADS = D_MODEL // 512
D_ATTN = N_HEADS * V_DIM
CONV_CH = D_MODEL - D_ATTN
QK_COLS = N_HEADS * 2 * HEAD_DIM
CONV_W = 31
D_FF = 11008
NORM_EPS = 1e-6
LN_EPS = 1e-5
PAGE_SIZE = 128
ATTN_SCALE = HEAD_DIM ** -0.5
LOG2E = math.log2(math.e)
NEG_BIG = -1e30
SA_COLS = 128

FF_TILE = 1024
D_FF_PAD = -(-D_FF // FF_TILE) * FF_TILE
SUBLANES = 8
ROW_CHUNK = 64
EPI_ROWS = 16
HALO = 32
V7X_VMEM_LIMIT = 56 * 1024 * 1024


def _lambda_init(layer):
    return 0.8 - 0.6 * math.exp(-0.3 * layer)


def _params(*sem):
    return pltpu.CompilerParams(dimension_semantics=sem, vmem_limit_bytes=V7X_VMEM_LIMIT)


def _rms(x, g):
    ms = jnp.mean(x * x, axis=-1, keepdims=True)
    return x * lax.rsqrt(ms + NORM_EPS) * g


def _cast_pad_cols_kernel(w_ref, o_ref):
    n = w_ref.shape[1]
    o_ref[:, :n] = w_ref[...].astype(BF16)
    o_ref[:, n:] = jnp.zeros((o_ref.shape[0], o_ref.shape[1] - n), BF16)


def cast_pad_cols(w, n_pad, tr=256):
    r, n = w.shape
    return pl.pallas_call(
        _cast_pad_cols_kernel,
        grid=(r // tr,),
        in_specs=[pl.BlockSpec((tr, n), lambda i: (i, 0))],
        out_specs=pl.BlockSpec((tr, n_pad), lambda i: (i, 0)),
        out_shape=jax.ShapeDtypeStruct((r, n_pad), BF16),
        compiler_params=_params("parallel"),
        name="cast_pad_cols",
    )(w)


def _cast_pad_rows_kernel(w_ref, o_ref, *, n_valid):
    i = pl.program_id(0)

    @pl.when(i < n_valid)
    def _():
        o_ref[...] = w_ref[...].astype(BF16)

    @pl.when(i >= n_valid)
    def _():
        o_ref[...] = jnp.zeros(o_ref.shape, BF16)


def cast_pad_rows(w, r_pad, n_blocks=16):
    r, n = w.shape
    tr = r // n_blocks
    assert tr * n_blocks == r and tr % 16 == 0 and r_pad - r <= tr
    return pl.pallas_call(
        functools.partial(_cast_pad_rows_kernel, n_valid=n_blocks),
        grid=(n_blocks + 1,),
        in_specs=[pl.BlockSpec((tr, n), lambda i: (jnp.minimum(i, n_blocks - 1), 0))],
        out_specs=pl.BlockSpec((tr, n), lambda i: (i, 0)),
        out_shape=jax.ShapeDtypeStruct((r_pad, n), BF16),
        compiler_params=_params("parallel"),
        name="cast_pad_rows",
    )(w)


def _prenorm_kernel(x_ref, g_ref, h_ref):
    h_ref[...] = _rms(x_ref[...], g_ref[...]).astype(BF16)


def prenorm(x, g, tm):
    m, d = x.shape
    return pl.pallas_call(
        _prenorm_kernel,
        grid=(m // tm,),
        in_specs=[pl.BlockSpec((tm, d), lambda i: (i, 0)),
                  pl.BlockSpec((1, d), lambda i: (0, 0))],
        out_specs=pl.BlockSpec((tm, d), lambda i: (i, 0)),
        out_shape=jax.ShapeDtypeStruct((m, d), BF16),
        compiler_params=_params("parallel"),
        name="prenorm",
    )(x, g)


def _gateup_kernel(h_ref, wg_ref, wu_ref, a_ref):
    h = h_ref[...]
    g = jnp.dot(h, wg_ref[...], preferred_element_type=F32)
    u = jnp.dot(h, wu_ref[...], preferred_element_type=F32)
    a_ref[...] = (g * jax.nn.sigmoid(g) * u).astype(BF16)


def gateup(h, wg, wu, tm, tn):
    m, d = h.shape
    f = wg.shape[1]
    return pl.pallas_call(
        _gateup_kernel,
        grid=(m // tm, f // tn),
        in_specs=[pl.BlockSpec((tm, d), lambda i, j: (i, 0)),
                  pl.BlockSpec((d, tn), lambda i, j: (0, j)),
                  pl.BlockSpec((d, tn), lambda i, j: (0, j))],
        out_specs=pl.BlockSpec((tm, tn), lambda i, j: (i, j)),
        out_shape=jax.ShapeDtypeStruct((m, f), BF16),
        compiler_params=_params("parallel", "arbitrary"),
        name="gateup",
    )(h, wg, wu)


def _rowmm_kernel(*refs, n_a, res_scale, has_next):
    a_refs = refs[:n_a]
    w_refs = refs[n_a:2 * n_a]
    x_ref, gpost_ref = refs[2 * n_a], refs[2 * n_a + 1]
    if has_next:
        gnext_ref, xo_ref, ho_ref = refs[2 * n_a + 2:]
    else:
        (xo_ref,) = refs[2 * n_a + 2:]
    k = pl.program_id(1)

    @pl.when(k == 0)
    def _():
        xo_ref[...] = jnp.zeros(xo_ref.shape, F32)

    for a_ref, w_ref in zip(a_refs, w_refs):
        xo_ref[...] += jnp.dot(a_ref[...], w_ref[...], preferred_element_type=F32)

    @pl.when(k == pl.num_programs(1) - 1)
    def _():
        def rows(r, carry):
            sl = pl.ds(pl.multiple_of(r * EPI_ROWS, EPI_ROWS), EPI_ROWS)
            xn = x_ref[sl, :] + res_scale * _rms(xo_ref[sl, :], gpost_ref[...])
            xo_ref[sl, :] = xn
            if has_next:
                ho_ref[sl, :] = _rms(xn, gnext_ref[...]).astype(BF16)
            return carry

        lax.fori_loop(0, xo_ref.shape[0] // EPI_ROWS, rows, 0, unroll=2)


def rowmm(a_list, w_list, x, g_post, g_next, res_scale, tm, tk, w_row_offsets=None):
    m, d = x.shape
    kdim = a_list[0].shape[1]
    n_a = len(a_list)
    has_next = g_next is not None
    w_row_offsets = w_row_offsets or (0,) * n_a

    def w_spec(row_offset):
        return pl.BlockSpec((tk, d), lambda i, k: (k + row_offset // tk, 0))

    in_specs = ([pl.BlockSpec((tm, tk), lambda i, k: (i, k)) for _ in a_list]
                + [w_spec(off) for off in w_row_offsets]
                + [pl.BlockSpec((tm, d), lambda i, k: (i, 0), pipeline_mode=pl.Buffered(1)),
                   pl.BlockSpec((1, d), lambda i, k: (0, 0))])
    args = list(a_list) + list(w_list) + [x, g_post]
    out_specs = [pl.BlockSpec((tm, d), lambda i, k: (i, 0))]
    out_shape = [jax.ShapeDtypeStruct((m, d), F32)]
    if has_next:
        in_specs.append(pl.BlockSpec((1, d), lambda i, k: (0, 0)))
        args.append(g_next)
        out_specs.append(pl.BlockSpec((tm, d), lambda i, k: (i, 0)))
        out_shape.append(jax.ShapeDtypeStruct((m, d), BF16))
    outs = pl.pallas_call(
        functools.partial(_rowmm_kernel, n_a=n_a, res_scale=res_scale, has_next=has_next),
        grid=(m // tm, kdim // tk),
        in_specs=in_specs,
        out_specs=out_specs,
        out_shape=out_shape,
        compiler_params=_params("parallel", "arbitrary"),
        name="rowmm",
    )(*args)
    return outs if has_next else (outs[0], None)


def _proj_q_kernel(h_ref, w_ref, q_ref):
    q = jnp.dot(h_ref[...], w_ref[...], preferred_element_type=F32)
    q_ref[...] = (q * (ATTN_SCALE * LOG2E)).astype(BF16)


def _proj_kv_kernel(h_ref, wk_ref, wv_ref, krows_ref, v32_ref, kb_ref, vb_ref):
    j = pl.program_id(1)
    h = h_ref[...]
    k = jnp.dot(h, wk_ref[...], preferred_element_type=F32)
    v = jnp.dot(h, wv_ref[...], preferred_element_type=F32)
    tm, tn = k.shape
    slabs = krows_ref.shape[0] // tm
    for q in range(tn // HEAD_DIM):
        krows_ref[pl.ds(j * (tn // HEAD_DIM) + q, tm, stride=slabs), :] = k[:, q * HEAD_DIM:(q + 1) * HEAD_DIM]
    v32_ref[...] = v
    kb_ref[...] = k.astype(BF16)
    vb_ref[...] = v.astype(BF16)


def _proj_glu_kernel(h_ref, wa_ref, wg_ref, o_ref):
    h = h_ref[...]
    a = jnp.dot(h, wa_ref[...], preferred_element_type=F32)
    g = jnp.dot(h, wg_ref[...], preferred_element_type=F32)
    o_ref[...] = a * jax.nn.sigmoid(g)


def in_proj(h, w_in, tm, tn, tn_kv):
    m, d = h.shape
    nq = QK_COLS // tn
    nv = D_ATTN // tn
    nc = CONV_CH // tn
    h_spec = pl.BlockSpec((tm, d), lambda i, j: (i, 0))
    o_spec = pl.BlockSpec((tm, tn), lambda i, j: (i, j))

    def w_spec(off):
        return pl.BlockSpec((d, tn), lambda i, j: (0, j + off))

    qb = pl.pallas_call(
        _proj_q_kernel, grid=(m // tm, nq),
        in_specs=[h_spec, w_spec(0)], out_specs=o_spec,
        out_shape=jax.ShapeDtypeStruct((m, QK_COLS), BF16),
        compiler_params=_params("parallel", "arbitrary"), name="proj_q",
    )(h, w_in)
    slabs = QK_COLS // HEAD_DIM
    kv_spec = pl.BlockSpec((tm, tn_kv), lambda i, j: (i, j))

    def kv_w_spec(col_offset):
        return pl.BlockSpec((d, tn_kv), lambda i, j: (0, j + col_offset // tn_kv))

    k_rows, v32, kb, vb = pl.pallas_call(
        _proj_kv_kernel, grid=(m // tm, QK_COLS // tn_kv),
        in_specs=[h_spec, kv_w_spec(QK_COLS), kv_w_spec(2 * QK_COLS)],
        out_specs=[pl.BlockSpec((tm * slabs, HEAD_DIM), lambda i, j: (i, 0)), kv_spec, kv_spec, kv_spec],
        out_shape=[jax.ShapeDtypeStruct((m * slabs, HEAD_DIM), F32), jax.ShapeDtypeStruct((m, D_ATTN), F32),
                   jax.ShapeDtypeStruct((m, QK_COLS), BF16), jax.ShapeDtypeStruct((m, D_ATTN), BF16)],
        compiler_params=_params("parallel", "arbitrary"), name="proj_kv",
    )(h, w_in, w_in)
    glu = pl.pallas_call(
        _proj_glu_kernel, grid=(m // tm, nc),
        in_specs=[h_spec, w_spec(2 * nq + nv), w_spec(2 * nq + nv + nc)], out_specs=o_spec,
        out_shape=jax.ShapeDtypeStruct((m, CONV_CH), F32),
        compiler_params=_params("parallel", "arbitrary"), name="proj_glu",
    )(h, w_in, w_in)
    return qb, k_rows, v32, kb, vb, glu


def _lambda_full(lam_ref, lam_init):
    lv = lam_ref[...]
    e1 = jnp.exp(jnp.sum(lv[0:1] * lv[1:2], axis=-1, keepdims=True))
    e2 = jnp.exp(jnp.sum(lv[2:3] * lv[3:4], axis=-1, keepdims=True))
    return e1 - e2 + lam_init


def _prompt_attn_kernel(slopes_ref, lam_ref, subg_ref, q_ref, k_ref, v_ref, o_ref,
                        b0_ref, s_ref, p_ref, alpha_ref, m_ref, l_ref, acc_ref, *, tq, tk, lam_init):
    h = pl.program_id(0)
    qi = pl.program_id(1)
    slope2 = slopes_ref[h] * LOG2E
    nt = (((1,), (1,)), ((), ()))

    def rel_local(n_rows):
        return lax.broadcasted_iota(jnp.int32, (n_rows, tk), 0) - lax.broadcasted_iota(jnp.int32, (n_rows, tk), 1)

    b0_ref[...] = -slope2 * rel_local(tq).astype(F32)
    m_ref[...] = jnp.full(m_ref.shape, NEG_BIG, F32)
    l_ref[...] = jnp.zeros(l_ref.shape, F32)
    acc_ref[...] = jnp.zeros(acc_ref.shape, F32)

    def scores(kj, slot):
        kb = k_ref[pl.ds(pl.multiple_of(kj * tk, tk), tk), :]
        for c in range(2):
            cols = slice(c * HEAD_DIM, (c + 1) * HEAD_DIM)
            s_ref[slot, c] = lax.dot_general(q_ref[:, cols], kb[:, cols], nt, preferred_element_type=F32)

    def softmax(kj, slot, on_diagonal=False):
        c_blk = -slope2 * ((qi - kj) * tq).astype(F32)
        for c in range(2):
            for r in range(0, tq, ROW_CHUNK):
                rows = slice(r, r + ROW_CHUNK)
                s = s_ref[slot, c, rows, :] + b0_ref[rows, :]
                if on_diagonal:
                    s = jnp.where(rel_local(ROW_CHUNK) >= -r, s, NEG_BIG)
                m_prev = m_ref[c, rows, :]
                m_next = jnp.maximum(m_prev, jnp.max(s, axis=1, keepdims=True) + c_blk)
                alpha = jnp.exp2(m_prev - m_next)
                p = jnp.exp2(s - pltpu.repeat(m_next - c_blk, tk // 128, axis=1))
                l_ref[c, rows, :] = alpha * l_ref[c, rows, :] + jnp.sum(p, axis=1, keepdims=True)
                m_ref[c, rows, :] = m_next
                alpha_ref[slot, c, rows, :] = alpha
                p_ref[slot, c, rows, :] = p.astype(BF16)

    def values(kj, slot):
        vb = v_ref[pl.ds(pl.multiple_of(jnp.maximum(kj, 0) * tk, tk), tk), :]
        for c in range(2):
            acc_ref[c] = (pltpu.repeat(alpha_ref[slot, c], V_DIM // 128, axis=1) * acc_ref[c]
                          + jnp.dot(p_ref[slot, c], vb, preferred_element_type=F32))

    p_ref[1] = jnp.zeros(p_ref.shape[1:], BF16)
    alpha_ref[1] = jnp.ones(alpha_ref.shape[1:], F32)
    scores(0, 0)

    def pair(u, carry):
        t = 2 * u
        values(t - 1, 1)
        softmax(t, 0)
        scores(t + 1, 1)
        values(t, 0)
        softmax(t + 1, 1)
        scores(t + 2, 0)
        return carry

    lax.fori_loop(0, qi // 2, pair, 0)
    t = (qi // 2) * 2

    @pl.when(qi % 2 == 0)
    def _():
        values(t - 1, 1)
        softmax(t, 0, on_diagonal=True)
        values(t, 0)

    @pl.when(qi % 2 == 1)
    def _():
        values(t - 1, 1)
        softmax(t, 0)
        scores(t + 1, 1)
        values(t, 0)
        softmax(t + 1, 1, on_diagonal=True)
        values(t + 1, 1)

    lam = _lambda_full(lam_ref, lam_init)
    o = (acc_ref[0] / pltpu.repeat(l_ref[0], V_DIM // 128, axis=1)
         - lam * (acc_ref[1] / pltpu.repeat(l_ref[1], V_DIM // 128, axis=1)))
    o_ref[...] = (_rms(o, subg_ref[...]) * (1.0 - lam_init)).astype(BF16)


def prompt_attention(qb, kb, vb, slopes, lam_vecs, subg, lam_init, tq, tk):
    t = qb.shape[0]
    assert tq == tk, "the kernel's diagonal handling assumes square tiles"
    return pl.pallas_call(
        functools.partial(_prompt_attn_kernel, tq=tq, tk=tk, lam_init=lam_init),
        grid=(N_HEADS, t // tq),
        in_specs=[pl.BlockSpec(memory_space=pltpu.SMEM),
                  pl.BlockSpec((4, HEAD_DIM), lambda h, i: (0, 0)),
                  pl.BlockSpec((1, V_DIM), lambda h, i: (0, 0)),
                  pl.BlockSpec((tq, V_DIM), lambda h, i: (i, h)),
                  pl.BlockSpec((t, V_DIM), lambda h, i: (0, h)),
                  pl.BlockSpec((t, V_DIM), lambda h, i: (0, h))],
        out_specs=pl.BlockSpec((tq, V_DIM), lambda h, i: (i, h)),
        out_shape=jax.ShapeDtypeStruct((t, D_ATTN), BF16),
        scratch_shapes=[pltpu.VMEM((tq, tk), F32), pltpu.VMEM((2, 2, tq, tk), F32),
                        pltpu.VMEM((2, 2, tq, tk), BF16), pltpu.VMEM((2, 2, tq, 128), F32),
                        pltpu.VMEM((2, tq, 128), F32), pltpu.VMEM((2, tq, 128), F32),
                        pltpu.VMEM((2, tq, V_DIM), F32)],
        compiler_params=_params("parallel", "arbitrary"),
        name="prompt_attn",
    )(slopes, lam_vecs, subg, qb, kb, vb)


def _lane_to_rows(v):
    n = v.shape[1]
    return jnp.transpose(jnp.broadcast_to(v, (n, n)))


def _sample_attn_kernel(pt_ref, qt_ref, scol_ref, qpos_ref, lam_ref, subg_ref, knew_ref, vnew_ref, *rest,
                        n_pages, n_new, past_len, lam_init):
    k_refs = rest[:n_pages]
    v_refs = rest[n_pages:2 * n_pages]
    o_ref, mb_ref, m_ref, l_ref, acc_ref = rest[2 * n_pages:]
    g = pl.program_id(1)
    n_cols = 2 * n_new * N_HEADS
    page_rows = PAGE_SIZE * N_HEADS
    scol = scol_ref[...]
    qpos = qpos_ref[...]
    qt = qt_ref[...]

    def head_match(n_rows):
        row = lax.broadcasted_iota(jnp.int32, (n_rows, SA_COLS), 0)
        col = lax.broadcasted_iota(jnp.int32, (n_rows, SA_COLS), 1)
        return row // N_HEADS, col, ((row % N_HEADS) == (col % N_HEADS)) & (col < n_cols)

    @pl.when(g == 0)
    def _():
        tok, _, ok = head_match(page_rows)
        mb_ref[...] = jnp.where(ok, scol * tok.astype(F32), NEG_BIG)
        m_ref[...] = jnp.full(m_ref.shape, NEG_BIG, F32)
        l_ref[...] = jnp.zeros(l_ref.shape, F32)
        acc_ref[...] = jnp.zeros(acc_ref.shape, F32)

    def keys(k_ref, n_rows):
        return jnp.concatenate([k_ref[pl.ds(c, n_rows, stride=2), :] for c in range(2)], axis=1).astype(BF16)

    def update(tiles):
        m_prev = m_ref[...]
        m_next = m_prev
        for s, a, _ in tiles:
            m_next = jnp.maximum(m_next, jnp.max(s, axis=0, keepdims=True) + a)
        alpha = jnp.exp2(m_prev - m_next)
        l_new = alpha * l_ref[...]
        pv = None
        for s, a, v in tiles:
            p = jnp.exp2(s - (m_next - a))
            l_new = l_new + jnp.sum(p, axis=0, keepdims=True)
            d = lax.dot_general(p.astype(BF16), v.astype(BF16), (((0,), (0,)), ((), ())),
                                preferred_element_type=F32)
            pv = d if pv is None else pv + d
        l_ref[...] = l_new
        m_ref[...] = m_next
        a_rows = _lane_to_rows(alpha)
        acc_ref[...] = jnp.concatenate([a_rows] * (V_DIM // SA_COLS), axis=1) * acc_ref[...] + pv

    tiles = []
    for p in range(n_pages):
        base = ((g * n_pages + p) * PAGE_SIZE).astype(F32)
        s = jnp.dot(keys(k_refs[p], page_rows), qt, preferred_element_type=F32) + mb_ref[...]
        tiles.append((s, -scol * (qpos - base), v_refs[p][...]))
    update(tiles)

    @pl.when(g == pl.num_programs(1) - 1)
    def _():
        new_rows = n_new * N_HEADS
        t_key, col, ok = head_match(new_rows)
        ok = ok & (t_key <= (col // N_HEADS) % n_new)
        s = jnp.dot(keys(knew_ref, new_rows), qt, preferred_element_type=F32)
        s = jnp.where(ok, s + scol * t_key.astype(F32), NEG_BIG)
        update([(s, -scol * (qpos - past_len), vnew_ref[...])])

        lam = _lambda_full(lam_ref, lam_init)
        l_rows = _lane_to_rows(l_ref[...])
        pn = acc_ref[...] / jnp.concatenate([l_rows] * (V_DIM // SA_COLS), axis=1)
        d = pn[:new_rows] - lam * pn[new_rows:2 * new_rows]
        o_ref[...] = _rms(d, subg_ref[...]) * (1.0 - lam_init)


def sample_attention(qt, scol, qpos, lam_vecs, subg, knew, vnew, cache_k2d, cache_v2d, page_table,
                     lam_init, n_new, n_pages):
    nb = qt.shape[0]
    pages_per_seq = page_table.shape[1]
    past_len = pages_per_seq * PAGE_SIZE
    new_rows = n_new * N_HEADS

    def page_spec(p, rows, width):
        return pl.BlockSpec((rows, width), lambda b, g, pt: (pt[b, g * n_pages + p], 0))

    const2 = lambda b, g, pt: (0, 0)
    per_b = lambda b, g, pt: (b, 0, 0)
    grid_spec = pltpu.PrefetchScalarGridSpec(
        num_scalar_prefetch=1,
        grid=(nb, pages_per_seq // n_pages),
        in_specs=[pl.BlockSpec((None, 2 * HEAD_DIM, SA_COLS), per_b),
                  pl.BlockSpec((1, SA_COLS), const2),
                  pl.BlockSpec((1, SA_COLS), const2),
                  pl.BlockSpec((4, HEAD_DIM), const2),
                  pl.BlockSpec((1, V_DIM), const2),
                  pl.BlockSpec((None, 2 * new_rows, HEAD_DIM), per_b),
                  pl.BlockSpec((None, new_rows, V_DIM), per_b)]
                 + [page_spec(p, 2 * PAGE_SIZE * N_HEADS, HEAD_DIM) for p in range(n_pages)]
                 + [page_spec(p, PAGE_SIZE * N_HEADS, V_DIM) for p in range(n_pages)],
        out_specs=pl.BlockSpec((None, new_rows, V_DIM), per_b),
        scratch_shapes=[pltpu.VMEM((PAGE_SIZE * N_HEADS, SA_COLS), F32),
                        pltpu.VMEM((1, SA_COLS), F32), pltpu.VMEM((1, SA_COLS), F32),
                        pltpu.VMEM((SA_COLS, V_DIM), F32)],
    )
    return pl.pallas_call(
        functools.partial(_sample_attn_kernel, n_pages=n_pages, n_new=n_new, past_len=past_len,
                          lam_init=lam_init),
        grid_spec=grid_spec,
        out_shape=jax.ShapeDtypeStruct((nb, new_rows, V_DIM), F32),
        compiler_params=_params("parallel", "arbitrary"),
        name="sample_attn",
    )(page_table, qt, scol, qpos, lam_vecs, subg, knew, vnew,
      *([cache_k2d] * n_pages), *([cache_v2d] * n_pages))


def _ln_swish(y, g, b):
    mu = jnp.mean(y, axis=-1, keepdims=True)
    yc = y - mu
    var = jnp.mean(yc * yc, axis=-1, keepdims=True)
    z = yc * lax.rsqrt(var + LN_EPS) * g + b
    return z * jax.nn.sigmoid(z)


def _prompt_conv_kernel(hist_ref, halo_ref, cur_ref, cw_ref, cb_ref, lg_ref, lb_ref, o_ref, buf_ref, y_ref,
                        *, tt, rb, lanes):
    i = pl.program_id(0)
    use_prev = jnp.broadcast_to(i > 0, halo_ref.shape)
    buf_ref[0:HALO, :] = jnp.where(use_prev, halo_ref[...], hist_ref[...])
    buf_ref[HALO:HALO + tt, :] = cur_ref[...]
    buf_ref[HALO + tt:, :] = jnp.zeros((SUBLANES, buf_ref.shape[1]), F32)
    shift = HALO - (CONV_W - 1)

    def chunk(cc, carry):
        c0 = pl.multiple_of(cc * lanes, lanes)
        cols = pl.ds(c0, lanes)
        for r0 in range(0, tt, rb):
            acc = jnp.broadcast_to(cb_ref[:, cols], (rb, lanes))
            for delta in range(SUBLANES):
                part = None
                for w in range(CONV_W):
                    if (w + shift) % SUBLANES != delta:
                        continue
                    term = (buf_ref[pl.ds(r0 + w + shift - delta, rb + SUBLANES), cols]
                            * cw_ref[pl.ds(w, 1), cols])
                    part = term if part is None else part + term
                if part is not None:
                    acc = acc + part[delta:delta + rb]
            y_ref[pl.ds(r0, rb), cols] = acc
        return carry

    lax.fori_loop(0, CONV_CH // lanes, chunk, 0)
    o_ref[...] = _ln_swish(y_ref[...], lg_ref[...], lb_ref[...]).astype(BF16)


def prompt_conv(glu, hist, cw, cb, lg, lb, tt):
    t, c = glu.shape
    const = lambda i: (0, 0)
    return pl.pallas_call(
        functools.partial(_prompt_conv_kernel, tt=tt, rb=64, lanes=128),
        grid=(t // tt,),
        in_specs=[pl.BlockSpec((HALO, c), const),
                  pl.BlockSpec((HALO, c), lambda i: (jnp.maximum(i * (tt // HALO) - 1, 0), 0)),
                  pl.BlockSpec((tt, c), lambda i: (i, 0)),
                  pl.BlockSpec((CONV_W, c), const),
                  pl.BlockSpec((1, c), const), pl.BlockSpec((1, c), const), pl.BlockSpec((1, c), const)],
        out_specs=pl.BlockSpec((tt, c), lambda i: (i, 0)),
        out_shape=jax.ShapeDtypeStruct((t, c), BF16),
        scratch_shapes=[pltpu.VMEM((HALO + tt + SUBLANES, c), F32), pltpu.VMEM((tt, c), F32)],
        compiler_params=_params("arbitrary"),
        name="prompt_conv",
    )(hist, glu, glu, cw, cb, lg, lb)


def _sample_conv_kernel(hist_ref, glu_ref, cw_ref, cb_ref, lg_ref, lb_ref, o_ref, nh_ref, full_ref, *, n_new):
    nh = CONV_W - 1
    full_ref[0:nh, :] = hist_ref[0]
    full_ref[nh:nh + n_new, :] = glu_ref[0]
    acc = jnp.broadcast_to(cb_ref[...], (n_new, cb_ref.shape[1]))
    for w in range(CONV_W):
        acc = acc + full_ref[w:w + n_new, :] * cw_ref[w:w + 1, :]
    o_ref[0] = _ln_swish(acc, lg_ref[...], lb_ref[...])
    nh_ref[0] = full_ref[n_new:n_new + nh, :]


def sample_conv(glu, hist, cw, cb, lg, lb):
    nb, n_new, c = glu.shape
    nh = CONV_W - 1
    const = lambda b: (0, 0)
    per_b = lambda b: (b, 0, 0)
    return pl.pallas_call(
        functools.partial(_sample_conv_kernel, n_new=n_new),
        grid=(nb,),
        in_specs=[pl.BlockSpec((1, nh, c), per_b), pl.BlockSpec((1, n_new, c), per_b),
                  pl.BlockSpec((CONV_W, c), const),
                  pl.BlockSpec((1, c), const), pl.BlockSpec((1, c), const), pl.BlockSpec((1, c), const)],
        out_specs=[pl.BlockSpec((1, n_new, c), per_b), pl.BlockSpec((1, nh, c), per_b)],
        out_shape=[jax.ShapeDtypeStruct((nb, n_new, c), F32), jax.ShapeDtypeStruct((nb, nh, c), F32)],
        scratch_shapes=[pltpu.VMEM((nh + n_new + 6, c), F32)],
        compiler_params=_params("parallel"),
        name="sample_conv",
    )(hist, glu, cw, cb, lg, lb)


def _query_columns(qb, nb, n_new):
    w = n_new * N_HEADS
    q = qb.reshape(nb, n_new, N_HEADS, 2, HEAD_DIM).transpose(0, 3, 4, 1, 2).reshape(nb, 2, HEAD_DIM, w)
    zeros = lambda n: jnp.zeros((nb, HEAD_DIM, n), qb.dtype)
    top = jnp.concatenate([q[:, 0], zeros(SA_COLS - w)], axis=-1)
    bot = jnp.concatenate([zeros(w), q[:, 1], zeros(SA_COLS - 2 * w)], axis=-1)
    return jnp.concatenate([top, bot], axis=1)


def _layer(x_p, x_s, ck, cv, hist_s, page_table, w, lam_init, slopes):
    t_p = x_p.shape[0]
    nb, n_new = hist_s.shape[0], x_s.shape[0] // hist_s.shape[0]
    tm_p, tm_s = 512, x_s.shape[0]
    outs = {}
    for name, x, tm, tm_big in (("p", x_p, tm_p, 1024), ("s", x_s, tm_s, tm_s)):
        h = prenorm(x, w["ffn1_pre"], tm)
        a = gateup(h, w["ffn1_wg"], w["ffn1_wu"], tm_big, 512)
        x, h = rowmm([a], [w["ffn1_wd"]], x, w["ffn1_post"], w["mix_pre"], 0.5, tm, 1024)
        qb, k32, v32, kb, vb, glu = in_proj(h, w["w_in"], tm_big, 512, 256)
        if name == "p":
            o = prompt_attention(qb, kb, vb, slopes, w["lam_vecs"], w["subln"], lam_init, 512, 512)
            hist0 = jnp.zeros((HALO, CONV_CH), F32)
            c = prompt_conv(glu, hist0, w["conv_w"], w["conv_b"], w["conv_ln_g"], w["conv_ln_b"], 256)
            new_hist = glu[t_p - (CONV_W - 1):][None]
        else:
            col = jnp.arange(SA_COLS)
            used = col < 2 * n_new * N_HEADS
            scol = jnp.where(used, slopes[col % N_HEADS] * LOG2E, 0.0)[None, :]
            qpos = (page_table.shape[1] * PAGE_SIZE + (col // N_HEADS) % n_new).astype(F32)[None, :]
            o = sample_attention(_query_columns(qb, nb, n_new), scol, qpos, w["lam_vecs"], w["subln"],
                                 k32.reshape(nb, n_new * N_HEADS * 2, HEAD_DIM),
                                 v32.reshape(nb, n_new * N_HEADS, V_DIM),
                                 ck.reshape(-1, HEAD_DIM), cv.reshape(-1, V_DIM),
                                 page_table, lam_init, n_new, 8)
            o = o.reshape(nb * n_new, D_ATTN).astype(BF16)
            c, new_hist = sample_conv(glu.reshape(nb, n_new, CONV_CH), hist_s, w["conv_w"], w["conv_b"],
                                      w["conv_ln_g"], w["conv_ln_b"])
            c = c.reshape(nb * n_new, CONV_CH).astype(BF16)
        x, h = rowmm([o, c], [w["w_out"], w["w_out"]], x, w["mix_post"], w["ffn2_pre"], 1.0, tm, 512,
                     w_row_offsets=(0, D_ATTN))
        a = gateup(h, w["ffn2_wg"], w["ffn2_wu"], tm_big, 512)
        x, _ = rowmm([a], [w["ffn2_wd"]], x, w["ffn2_post"], None, 0.5, tm, 1024)
        outs[name] = (x, k32, v32, new_hist)
    return outs


def kernel(x_prompt, x_sample, cache_k, cache_v, state_conv, page_table, ffn1_pre_g, ffn1_w_gate, ffn1_w_up, ffn1_w_down, ffn1_post_g, mix_pre_g, w_in, lambda_q1, lambda_k1, lambda_q2, lambda_k2, subln_g, conv_w, conv_b, conv_ln_g, conv_ln_b, w_out, mix_post_g, ffn2_pre_g, ffn2_w_gate, ffn2_w_up, ffn2_w_down, ffn2_post_g):
    depth = w_in.shape[0]
    batch, seq, _ = x_prompt.shape
    nb, n_new, _ = x_sample.shape
    slopes = 2.0 ** (-8.0 * jnp.arange(1, N_HEADS + 1, dtype=F32) / N_HEADS)
    up_w = functools.partial(cast_pad_cols, n_pad=D_FF_PAD)
    down_w = functools.partial(cast_pad_rows, r_pad=D_FF_PAD)
    row = lambda v: v.reshape(1, -1).astype(F32)
    assert batch == 1
    x_p = x_prompt.reshape(seq, D_MODEL)
    x_s = x_sample.reshape(nb * n_new, D_MODEL)
    kp, vp, cp, ks, vs, cs = [], [], [], [], [], []
    for l in range(depth):
        wo = w_out[l].astype(BF16)
        w = {
            "ffn1_pre": row(ffn1_pre_g[l]), "ffn1_wg": up_w(ffn1_w_gate[l]), "ffn1_wu": up_w(ffn1_w_up[l]),
            "ffn1_wd": down_w(ffn1_w_down[l]), "ffn1_post": row(ffn1_post_g[l]),
            "mix_pre": row(mix_pre_g[l]), "w_in": w_in[l].astype(BF16), "subln": row(subln_g[l]),
            "lam_vecs": jnp.stack([lambda_q1[l], lambda_k1[l], lambda_q2[l], lambda_k2[l]]).astype(F32),
            "conv_w": conv_w[l].astype(F32), "conv_b": row(conv_b[l]),
            "conv_ln_g": row(conv_ln_g[l]), "conv_ln_b": row(conv_ln_b[l]),
            "w_out": wo, "mix_post": row(mix_post_g[l]),
            "ffn2_pre": row(ffn2_pre_g[l]), "ffn2_wg": up_w(ffn2_w_gate[l]), "ffn2_wu": up_w(ffn2_w_up[l]),
            "ffn2_wd": down_w(ffn2_w_down[l]), "ffn2_post": row(ffn2_post_g[l]),
        }
        outs = _layer(x_p, x_s, cache_k[l], cache_v[l], state_conv[l], page_table, w, _lambda_init(l), slopes)
        x_p, k32, v32, nh = outs["p"]
        kp.append(k32.reshape(batch, seq, N_HEADS, 2, HEAD_DIM))
        vp.append(v32.reshape(batch, seq, N_HEADS, V_DIM))
        cp.append(nh)
        x_s, k32, v32, nh = outs["s"]
        ks.append(k32.reshape(nb, n_new, N_HEADS, 2, HEAD_DIM))
        vs.append(v32.reshape(nb, n_new, N_HEADS, V_DIM))
        cs.append(nh)
    return (x_p.reshape(batch, seq, D_MODEL), x_s.reshape(nb, n_new, D_MODEL),
            jnp.stack(kp), jnp.stack(vp), jnp.stack(cp), jnp.stack(ks), jnp.stack(vs), jnp.stack(cs))
```

```python
import functools
import math

import jax
import jax.numpy as jnp
from jax import lax
from jax.experimental import pallas as pl
from jax.experimental.pallas import tpu as pltpu

F32 = jnp.float32
BF16 = jnp.bfloat16

D_MODEL = 4096
HEAD_DIM = 128
V_DIM = 2 * HEAD_DIM
N_HEADS = D_MODEL // 512
D_ATTN = N_HEADS * V_DIM
CONV_CH = D_MODEL - D_ATTN
QK_COLS = N_HEADS * 2 * HEAD_DIM
CONV_W = 31
D_FF = 11008
NORM_EPS = 1e-6
LN_EPS = 1e-5
PAGE_SIZE = 128
ATTN_SCALE = HEAD_DIM ** -0.5
LOG2E = math.log2(math.e)
NEG_BIG = -1e30
SA_COLS = 128

FF_TILE = 1024
D_FF_PAD = -(-D_FF // FF_TILE) * FF_TILE
SUBLANES = 8
ROW_CHUNK = 64
EPI_ROWS = 16
HALO = 32
V7X_VMEM_LIMIT = 56 * 1024 * 1024


def _lambda_init(layer):
    return 0.8 - 0.6 * math.exp(-0.3 * layer)


def _params(*sem):
    return pltpu.CompilerParams(dimension_semantics=sem, vmem_limit_bytes=V7X_VMEM_LIMIT)


def _rms(x, g):
    ms = jnp.mean(x * x, axis=-1, keepdims=True)
    return x * lax.rsqrt(ms + NORM_EPS) * g


def _cast_pad_cols_kernel(w_ref, o_ref):
    n = w_ref.shape[1]
    o_ref[:, :n] = w_ref[...].astype(BF16)
    o_ref[:, n:] = jnp.zeros((o_ref.shape[0], o_ref.shape[1] - n), BF16)


def cast_pad_cols(w, n_pad, tr=256):
    r, n = w.shape
    return pl.pallas_call(
        _cast_pad_cols_kernel,
        grid=(r // tr,),
        in_specs=[pl.BlockSpec((tr, n), lambda i: (i, 0))],
        out_specs=pl.BlockSpec((tr, n_pad), lambda i: (i, 0)),
        out_shape=jax.ShapeDtypeStruct((r, n_pad), BF16),
        compiler_params=_params("parallel"),
        name="cast_pad_cols",
    )(w)


def _cast_pad_rows_kernel(w_ref, o_ref, *, n_valid):
    i = pl.program_id(0)

    @pl.when(i < n_valid)
    def _():
        o_ref[...] = w_ref[...].astype(BF16)

    @pl.when(i >= n_valid)
    def _():
        o_ref[...] = jnp.zeros(o_ref.shape, BF16)


def cast_pad_rows(w, r_pad, n_blocks=16):
    r, n = w.shape
    tr = r // n_blocks
    assert tr * n_blocks == r and tr % 16 == 0 and r_pad - r <= tr
    return pl.pallas_call(
        functools.partial(_cast_pad_rows_kernel, n_valid=n_blocks),
        grid=(n_blocks + 1,),
        in_specs=[pl.BlockSpec((tr, n), lambda i: (jnp.minimum(i, n_blocks - 1), 0))],
        out_specs=pl.BlockSpec((tr, n), lambda i: (i, 0)),
        out_shape=jax.ShapeDtypeStruct((r_pad, n), BF16),
        compiler_params=_params("parallel"),
        name="cast_pad_rows",
    )(w)


def _prenorm_kernel(x_ref, g_ref, h_ref):
    h_ref[...] = _rms(x_ref[...], g_ref[...]).astype(BF16)


def prenorm(x, g, tm):
    m, d = x.shape
    return pl.pallas_call(
        _prenorm_kernel,
        grid=(m // tm,),
        in_specs=[pl.BlockSpec((tm, d), lambda i: (i, 0)),
                  pl.BlockSpec((1, d), lambda i: (0, 0))],
        out_specs=pl.BlockSpec((tm, d), lambda i: (i, 0)),
        out_shape=jax.ShapeDtypeStruct((m, d), BF16),
        compiler_params=_params("parallel"),
        name="prenorm",
    )(x, g)


def _gateup_kernel(h_ref, wg_ref, wu_ref, a_ref):
    h = h_ref[...]
    g = jnp.dot(h, wg_ref[...], preferred_element_type=F32)
    u = jnp.dot(h, wu_ref[...], preferred_element_type=F32)
    a_ref[...] = (g * jax.nn.sigmoid(g) * u).astype(BF16)


def gateup(h, wg, wu, tm, tn):
    m, d = h.shape
    f = wg.shape[1]
    return pl.pallas_call(
        _gateup_kernel,
        grid=(m // tm, f // tn),
        in_specs=[pl.BlockSpec((tm, d), lambda i, j: (i, 0)),
                  pl.BlockSpec((d, tn), lambda i, j: (0, j)),
                  pl.BlockSpec((d, tn), lambda i, j: (0, j))],
        out_specs=pl.BlockSpec((tm, tn), lambda i, j: (i, j)),
        out_shape=jax.ShapeDtypeStruct((m, f), BF16),
        compiler_params=_params("parallel", "arbitrary"),
        name="gateup",
    )(h, wg, wu)


def _rowmm_kernel(*refs, n_a, res_scale, has_next):
    a_refs = refs[:n_a]
    w_refs = refs[n_a:2 * n_a]
    x_ref, gpost_ref = refs[2 * n_a], refs[2 * n_a + 1]
    if has_next:
        gnext_ref, xo_ref, ho_ref = refs[2 * n_a + 2:]
    else:
        (xo_ref,) = refs[2 * n_a + 2:]
    k = pl.program_id(1)

    @pl.when(k == 0)
    def _():
        xo_ref[...] = jnp.dot(a_refs[0][...], w_refs[0][...], preferred_element_type=F32)
        for a_ref, w_ref in zip(a_refs[1:], w_refs[1:]):
            xo_ref[...] += jnp.dot(a_ref[...], w_ref[...], preferred_element_type=F32)

    @pl.when(k > 0)
    def _():
        for a_ref, w_ref in zip(a_refs, w_refs):
            xo_ref[...] += jnp.dot(a_ref[...], w_ref[...], preferred_element_type=F32)

    @pl.when(k == pl.num_programs(1) - 1)
    def _():
        def rows(r, carry):
            sl = pl.ds(pl.multiple_of(r * EPI_ROWS, EPI_ROWS), EPI_ROWS)
            xn = x_ref[sl, :] + res_scale * _rms(xo_ref[sl, :], gpost_ref[...])
            xo_ref[sl, :] = xn
            if has_next:
                ho_ref[sl, :] = _rms(xn, gnext_ref[...]).astype(BF16)
            return carry

        lax.fori_loop(0, xo_ref.shape[0] // EPI_ROWS, rows, 0, unroll=2)


def rowmm(a_list, w_list, x, g_post, g_next, res_scale, tm, tk, w_row_offsets=None):
    m, d = x.shape
    kdim = a_list[0].shape[1]
    n_a = len(a_list)
    has_next = g_next is not None
    w_row_offsets = w_row_offsets or (0,) * n_a

    def w_spec(row_offset):
        return pl.BlockSpec((tk, d), lambda i, k: (k + row_offset // tk, 0))

    in_specs = ([pl.BlockSpec((tm, tk), lambda i, k: (i, k)) for _ in a_list]
                + [w_spec(off) for off in w_row_offsets]
                + [pl.BlockSpec((tm, d), lambda i, k: (i, 0), pipeline_mode=pl.Buffered(1)),
                   pl.BlockSpec((1, d), lambda i, k: (0, 0))])
    args = list(a_list) + list(w_list) + [x, g_post]
    out_specs = [pl.BlockSpec((tm, d), lambda i, k: (i, 0))]
    out_shape = [jax.ShapeDtypeStruct((m, d), F32)]
    if has_next:
        in_specs.append(pl.BlockSpec((1, d), lambda i, k: (0, 0)))
        args.append(g_next)
        out_specs.append(pl.BlockSpec((tm, d), lambda i, k: (i, 0)))
        out_shape.append(jax.ShapeDtypeStruct((m, d), BF16))
    outs = pl.pallas_call(
        functools.partial(_rowmm_kernel, n_a=n_a, res_scale=res_scale, has_next=has_next),
        grid=(m // tm, kdim // tk),
        in_specs=in_specs,
        out_specs=out_specs,
        out_shape=out_shape,
        compiler_params=_params("parallel", "arbitrary"),
        name="rowmm",
    )(*args)
    return outs if has_next else (outs[0], None)


def _proj_q_kernel(h_ref, w_ref, q_ref):
    q = jnp.dot(h_ref[...], w_ref[...], preferred_element_type=F32)
    q_ref[...] = (q * (ATTN_SCALE * LOG2E)).astype(BF16)


def _proj_kv_kernel(h_ref, wk_ref, wv_ref, krows_ref, v32_ref, kb_ref, vb_ref):
    j = pl.program_id(1)
    h = h_ref[...]
    k = jnp.dot(h, wk_ref[...], preferred_element_type=F32)
    v = jnp.dot(h, wv_ref[...], preferred_element_type=F32)
    tm, tn = k.shape
    slabs = krows_ref.shape[0] // tm
    for q in range(tn // HEAD_DIM):
        krows_ref[pl.ds(j * (tn // HEAD_DIM) + q, tm, stride=slabs), :] = k[:, q * HEAD_DIM:(q + 1) * HEAD_DIM]
    v32_ref[...] = v
    kb_ref[...] = k.astype(BF16)
    vb_ref[...] = v.astype(BF16)


def _proj_glu_kernel(h_ref, wa_ref, wg_ref, o_ref):
    h = h_ref[...]
    a = jnp.dot(h, wa_ref[...], preferred_element_type=F32)
    g = jnp.dot(h, wg_ref[...], preferred_element_type=F32)
    o_ref[...] = a * jax.nn.sigmoid(g)


def in_proj(h, w_in, tm, tn, tn_kv):
    m, d = h.shape
    nq = QK_COLS // tn
    nv = D_ATTN // tn
    nc = CONV_CH // tn
    h_spec = pl.BlockSpec((tm, d), lambda i, j: (i, 0))
    o_spec = pl.BlockSpec((tm, tn), lambda i, j: (i, j))

    def w_spec(off):
        return pl.BlockSpec((d, tn), lambda i, j: (0, j + off))

    qb = pl.pallas_call(
        _proj_q_kernel, grid=(m // tm, nq),
        in_specs=[h_spec, w_spec(0)], out_specs=o_spec,
        out_shape=jax.ShapeDtypeStruct((m, QK_COLS), BF16),
        compiler_params=_params("parallel", "arbitrary"), name="proj_q",
    )(h, w_in)
    slabs = QK_COLS // HEAD_DIM
    kv_spec = pl.BlockSpec((tm, tn_kv), lambda i, j: (i, j))

    def kv_w_spec(col_offset):
        return pl.BlockSpec((d, tn_kv), lambda i, j: (0, j + col_offset // tn_kv))

    k_rows, v32, kb, vb = pl.pallas_call(
        _proj_kv_kernel, grid=(m // tm, QK_COLS // tn_kv),
        in_specs=[h_spec, kv_w_spec(QK_COLS), kv_w_spec(2 * QK_COLS)],
        out_specs=[pl.BlockSpec((tm * slabs, HEAD_DIM), lambda i, j: (i, 0)), kv_spec, kv_spec, kv_spec],
        out_shape=[jax.ShapeDtypeStruct((m * slabs, HEAD_DIM), F32), jax.ShapeDtypeStruct((m, D_ATTN), F32),
                   jax.ShapeDtypeStruct((m, QK_COLS), BF16), jax.ShapeDtypeStruct((m, D_ATTN), BF16)],
        compiler_params=_params("parallel", "arbitrary"), name="proj_kv",
    )(h, w_in, w_in)
    glu = pl.pallas_call(
        _proj_glu_kernel, grid=(m // tm, nc),
        in_specs=[h_spec, w_spec(2 * nq + nv), w_spec(2 * nq + nv + nc)], out_specs=o_spec,
        out_shape=jax.ShapeDtypeStruct((m, CONV_CH), F32),
        compiler_params=_params("parallel", "arbitrary"), name="proj_glu",
    )(h, w_in, w_in)
    return qb, k_rows, v32, kb, vb, glu


def _lambda_full(lam_ref, lam_init):
    lv = lam_ref[...]
    e1 = jnp.exp(jnp.sum(lv[0:1] * lv[1:2], axis=-1, keepdims=True))
    e2 = jnp.exp(jnp.sum(lv[2:3] * lv[3:4], axis=-1, keepdims=True))
    return e1 - e2 + lam_init


def _prompt_attn_kernel(slopes_ref, lam_ref, subg_ref, q_ref, k_ref, v_ref, o_ref,
                        b0_ref, s_ref, p_ref, alpha_ref, m_ref, l_ref, acc_ref, *, tq, tk, lam_init):
    h = pl.program_id(0)
    qi = pl.program_id(1)
    slope2 = slopes_ref[h] * LOG2E
    nt = (((1,), (1,)), ((), ()))

    def rel_local(n_rows):
        return lax.broadcasted_iota(jnp.int32, (n_rows, tk), 0) - lax.broadcasted_iota(jnp.int32, (n_rows, tk), 1)

    b0_ref[...] = -slope2 * rel_local(tq).astype(F32)
    m_ref[...] = jnp.full(m_ref.shape, NEG_BIG, F32)
    l_ref[...] = jnp.zeros(l_ref.shape, F32)
    acc_ref[...] = jnp.zeros(acc_ref.shape, F32)

    def scores(kj, slot):
        kb = k_ref[pl.ds(pl.multiple_of(kj * tk, tk), tk), :]
        for c in range(2):
            cols = slice(c * HEAD_DIM, (c + 1) * HEAD_DIM)
            s_ref[slot, c] = lax.dot_general(q_ref[:, cols], kb[:, cols], nt, preferred_element_type=F32)

    def softmax(kj, slot, on_diagonal=False):
        c_blk = -slope2 * ((qi - kj) * tq).astype(F32)
        for c in range(2):
            for r in range(0, tq, ROW_CHUNK):
                rows = slice(r, r + ROW_CHUNK)
                s = s_ref[slot, c, rows, :] + b0_ref[rows, :]
                if on_diagonal:
                    s = jnp.where(rel_local(ROW_CHUNK) >= -r, s, NEG_BIG)
                m_prev = m_ref[c, rows, :]
                m_next = jnp.maximum(m_prev, jnp.max(s, axis=1, keepdims=True) + c_blk)
                alpha = jnp.exp2(m_prev - m_next)
                p = jnp.exp2(s - pltpu.repeat(m_next - c_blk, tk // 128, axis=1))
                l_ref[c, rows, :] = alpha * l_ref[c, rows, :] + jnp.sum(p, axis=1, keepdims=True)
                m_ref[c, rows, :] = m_next
                alpha_ref[slot, c, rows, :] = alpha
                p_ref[slot, c, rows, :] = p.astype(BF16)

    def values(kj, slot):
        vb = v_ref[pl.ds(pl.multiple_of(jnp.maximum(kj, 0) * tk, tk), tk), :]
        for c in range(2):
            acc_ref[c] = (pltpu.repeat(alpha_ref[slot, c], V_DIM // 128, axis=1) * acc_ref[c]
                          + jnp.dot(p_ref[slot, c], vb, preferred_element_type=F32))

    p_ref[1] = jnp.zeros(p_ref.shape[1:], BF16)
    alpha_ref[1] = jnp.ones(alpha_ref.shape[1:], F32)
    scores(0, 0)

    def pair(u, carry):
        t = 2 * u
        values(t - 1, 1)
        softmax(t, 0)
        scores(t + 1, 1)
        values(t, 0)
        softmax(t + 1, 1)
        scores(t + 2, 0)
        return carry

    lax.fori_loop(0, qi // 2, pair, 0)
    t = (qi // 2) * 2

    @pl.when(qi % 2 == 0)
    def _():
        values(t - 1, 1)
        softmax(t, 0, on_diagonal=True)
        values(t, 0)

    @pl.when(qi % 2 == 1)
    def _():
        values(t - 1, 1)
        softmax(t, 0)
        scores(t + 1, 1)
        values(t, 0)
        softmax(t + 1, 1, on_diagonal=True)
        values(t + 1, 1)

    lam = _lambda_full(lam_ref, lam_init)
    o = (acc_ref[0] / pltpu.repeat(l_ref[0], V_DIM // 128, axis=1)
         - lam * (acc_ref[1] / pltpu.repeat(l_ref[1], V_DIM // 128, axis=1)))
    o_ref[...] = (_rms(o, subg_ref[...]) * (1.0 - lam_init)).astype(BF16)


def prompt_attention(qb, kb, vb, slopes, lam_vecs, subg, lam_init, tq, tk):
    t = qb.shape[0]
    assert tq == tk, "the kernel's diagonal handling assumes square tiles"
    return pl.pallas_call(
        functools.partial(_prompt_attn_kernel, tq=tq, tk=tk, lam_init=lam_init),
        grid=(N_HEADS, t // tq),
        in_specs=[pl.BlockSpec(memory_space=pltpu.SMEM),
                  pl.BlockSpec((4, HEAD_DIM), lambda h, i: (0, 0)),
                  pl.BlockSpec((1, V_DIM), lambda h, i: (0, 0)),
                  pl.BlockSpec((tq, V_DIM), lambda h, i: (i, h)),
                  pl.BlockSpec((t, V_DIM), lambda h, i: (0, h)),
                  pl.BlockSpec((t, V_DIM), lambda h, i: (0, h))],
        out_specs=pl.BlockSpec((tq, V_DIM), lambda h, i: (i, h)),
        out_shape=jax.ShapeDtypeStruct((t, D_ATTN), BF16),
        scratch_shapes=[pltpu.VMEM((tq, tk), F32), pltpu.VMEM((2, 2, tq, tk), F32),
                        pltpu.VMEM((2, 2, tq, tk), BF16), pltpu.VMEM((2, 2, tq, 128), F32),
                        pltpu.VMEM((2, tq, 128), F32), pltpu.VMEM((2, tq, 128), F32),
                        pltpu.VMEM((2, tq, V_DIM), F32)],
        compiler_params=_params("parallel", "arbitrary"),
        name="prompt_attn",
    )(slopes, lam_vecs, subg, qb, kb, vb)


def _lane_to_rows(v):
    n = v.shape[1]
    return jnp.transpose(jnp.broadcast_to(v, (n, n)))


def _sample_attn_kernel(pt_ref, qt_ref, scol_ref, qpos_ref, lam_ref, subg_ref, knew_ref, vnew_ref, *rest,
                        n_pages, n_new, past_len, lam_init):
    k_refs = rest[:n_pages]
    v_refs = rest[n_pages:2 * n_pages]
    o_ref, mb_ref, m_ref, l_ref, acc_ref = rest[2 * n_pages:]
    g = pl.program_id(1)
    n_cols = 2 * n_new * N_HEADS
    page_rows = PAGE_SIZE * N_HEADS
    scol = scol_ref[...]
    qpos = qpos_ref[...]
    qt = qt_ref[...]

    def head_match(n_rows):
        row = lax.broadcasted_iota(jnp.int32, (n_rows, SA_COLS), 0)
        col = lax.broadcasted_iota(jnp.int32, (n_rows, SA_COLS), 1)
        return row // N_HEADS, col, ((row % N_HEADS) == (col % N_HEADS)) & (col < n_cols)

    @pl.when(g == 0)
    def _():
        tok, _, ok = head_match(page_rows)
        mb_ref[...] = jnp.where(ok, scol * tok.astype(F32), NEG_BIG)
        m_ref[...] = jnp.full(m_ref.shape, NEG_BIG, F32)
        l_ref[...] = jnp.zeros(l_ref.shape, F32)
        acc_ref[...] = jnp.zeros(acc_ref.shape, F32)

    def keys(k_ref, n_rows):
        return jnp.concatenate([k_ref[pl.ds(c, n_rows, stride=2), :] for c in range(2)], axis=1).astype(BF16)

    def update(tiles):
        m_prev = m_ref[...]
        m_next = m_prev
        for s, a, _ in tiles:
            m_next = jnp.maximum(m_next, jnp.max(s, axis=0, keepdims=True) + a)
        alpha = jnp.exp2(m_prev - m_next)
        l_new = alpha * l_ref[...]
        pv = None
        for s, a, v in tiles:
            p = jnp.exp2(s - (m_next - a))
            l_new = l_new + jnp.sum(p, axis=0, keepdims=True)
            d = lax.dot_general(p.astype(BF16), v.astype(BF16), (((0,), (0,)), ((), ())),
                                preferred_element_type=F32)
            pv = d if pv is None else pv + d
        l_ref[...] = l_new
        m_ref[...] = m_next
        a_rows = _lane_to_rows(alpha)
        acc_ref[...] = jnp.concatenate([a_rows] * (V_DIM // SA_COLS), axis=1) * acc_ref[...] + pv

    tiles = []
    for p in range(n_pages):
        base = ((g * n_pages + p) * PAGE_SIZE).astype(F32)
        s = jnp.dot(keys(k_refs[p], page_rows), qt, preferred_element_type=F32) + mb_ref[...]
        tiles.append((s, -scol * (qpos - base), v_refs[p][...]))
    update(tiles)

    @pl.when(g == pl.num_programs(1) - 1)
    def _():
        new_rows = n_new * N_HEADS
        t_key, col, ok = head_match(new_rows)
        ok = ok & (t_key <= (col // N_HEADS) % n_new)
        s = jnp.dot(keys(knew_ref, new_rows), qt, preferred_element_type=F32)
        s = jnp.where(ok, s + scol * t_key.astype(F32), NEG_BIG)
        update([(s, -scol * (qpos - past_len), vnew_ref[...])])

        lam = _lambda_full(lam_ref, lam_init)
        l_rows = _lane_to_rows(l_ref[...])
        pn = acc_ref[...] / jnp.concatenate([l_rows] * (V_DIM // SA_COLS), axis=1)
        d = pn[:new_rows] - lam * pn[new_rows:2 * new_rows]
        o_ref[...] = _rms(d, subg_ref[...]) * (1.0 - lam_init)


def sample_attention(qt, scol, qpos, lam_vecs, subg, knew, vnew, cache_k2d, cache_v2d, page_table,
                     lam_init, n_new, n_pages):
    nb = qt.shape[0]
    pages_per_seq = page_table.shape[1]
    past_len = pages_per_seq * PAGE_SIZE
    new_rows = n_new * N_HEADS

    def page_spec(p, rows, width):
        return pl.BlockSpec((rows, width), lambda b, g, pt: (pt[b, g * n_pages + p], 0))

    const2 = lambda b, g, pt: (0, 0)
    per_b = lambda b, g, pt: (b, 0, 0)
    grid_spec = pltpu.PrefetchScalarGridSpec(
        num_scalar_prefetch=1,
        grid=(nb, pages_per_seq // n_pages),
        in_specs=[pl.BlockSpec((None, 2 * HEAD_DIM, SA_COLS), per_b),
                  pl.BlockSpec((1, SA_COLS), const2),
                  pl.BlockSpec((1, SA_COLS), const2),
                  pl.BlockSpec((4, HEAD_DIM), const2),
                  pl.BlockSpec((1, V_DIM), const2),
                  pl.BlockSpec((None, 2 * new_rows, HEAD_DIM), per_b),
                  pl.BlockSpec((None, new_rows, V_DIM), per_b)]
                 + [page_spec(p, 2 * PAGE_SIZE * N_HEADS, HEAD_DIM) for p in range(n_pages)]
                 + [page_spec(p, PAGE_SIZE * N_HEADS, V_DIM) for p in range(n_pages)],
        out_specs=pl.BlockSpec((None, new_rows, V_DIM), per_b),
        scratch_shapes=[pltpu.VMEM((PAGE_SIZE * N_HEADS, SA_COLS), F32),
                        pltpu.VMEM((1, SA_COLS), F32), pltpu.VMEM((1, SA_COLS), F32),
                        pltpu.VMEM((SA_COLS, V_DIM), F32)],
    )
    return pl.pallas_call(
        functools.partial(_sample_attn_kernel, n_pages=n_pages, n_new=n_new, past_len=past_len,
                          lam_init=lam_init),
        grid_spec=grid_spec,
        out_shape=jax.ShapeDtypeStruct((nb, new_rows, V_DIM), F32),
        compiler_params=_params("parallel", "arbitrary"),
        name="sample_attn",
    )(page_table, qt, scol, qpos, lam_vecs, subg, knew, vnew,
      *([cache_k2d] * n_pages), *([cache_v2d] * n_pages))


def _ln_swish(y, g, b):
    mu = jnp.mean(y, axis=-1, keepdims=True)
    yc = y - mu
    var = jnp.mean(yc * yc, axis=-1, keepdims=True)
    z = yc * lax.rsqrt(var + LN_EPS) * g + b
    return z * jax.nn.sigmoid(z)


def _prompt_conv_kernel(hist_ref, halo_ref, cur_ref, cw_ref, cb_ref, lg_ref, lb_ref, o_ref, buf_ref, y_ref,
                        *, tt, rb, lanes):
    i = pl.program_id(0)
    use_prev = jnp.broadcast_to(i > 0, halo_ref.shape)
    buf_ref[0:HALO, :] = jnp.where(use_prev, halo_ref[...], hist_ref[...])
    buf_ref[HALO:HALO + tt, :] = cur_ref[...]
    buf_ref[HALO + tt:, :] = jnp.zeros((SUBLANES, buf_ref.shape[1]), F32)
    shift = HALO - (CONV_W - 1)

    def chunk(cc, carry):
        c0 = pl.multiple_of(cc * lanes, lanes)
        cols = pl.ds(c0, lanes)
        for r0 in range(0, tt, rb):
            acc = jnp.broadcast_to(cb_ref[:, cols], (rb, lanes))
            for delta in range(SUBLANES):
                part = None
                for w in range(CONV_W):
                    if (w + shift) % SUBLANES != delta:
                        continue
                    term = (buf_ref[pl.ds(r0 + w + shift - delta, rb + SUBLANES), cols]
                            * cw_ref[pl.ds(w, 1), cols])
                    part = term if part is None else part + term
                if part is not None:
                    acc = acc + part[delta:delta + rb]
            y_ref[pl.ds(r0, rb), cols] = acc
        return carry

    lax.fori_loop(0, CONV_CH // lanes, chunk, 0)
    o_ref[...] = _ln_swish(y_ref[...], lg_ref[...], lb_ref[...]).astype(BF16)


def prompt_conv(glu, hist, cw, cb, lg, lb, tt):
    t, c = glu.shape
    const = lambda i: (0, 0)
    return pl.pallas_call(
        functools.partial(_prompt_conv_kernel, tt=tt, rb=64, lanes=128),
        grid=(t // tt,),
        in_specs=[pl.BlockSpec((HALO, c), const),
                  pl.BlockSpec((HALO, c), lambda i: (jnp.maximum(i * (tt // HALO) - 1, 0), 0)),
                  pl.BlockSpec((tt, c), lambda i: (i, 0)),
                  pl.BlockSpec((CONV_W, c), const),
                  pl.BlockSpec((1, c), const), pl.BlockSpec((1, c), const), pl.BlockSpec((1, c), const)],
        out_specs=pl.BlockSpec((tt, c), lambda i: (i, 0)),
        out_shape=jax.ShapeDtypeStruct((t, c), BF16),
        scratch_shapes=[pltpu.VMEM((HALO + tt + SUBLANES, c), F32), pltpu.VMEM((tt, c), F32)],
        compiler_params=_params("arbitrary"),
        name="prompt_conv",
    )(hist, glu, glu, cw, cb, lg, lb)


def _sample_conv_kernel(hist_ref, glu_ref, cw_ref, cb_ref, lg_ref, lb_ref, o_ref, nh_ref, full_ref, *, n_new):
    nh = CONV_W - 1
    full_ref[0:nh, :] = hist_ref[0]
    full_ref[nh:nh + n_new, :] = glu_ref[0]
    acc = jnp.broadcast_to(cb_ref[...], (n_new, cb_ref.shape[1]))
    for w in range(CONV_W):
        acc = acc + full_ref[w:w + n_new, :] * cw_ref[w:w + 1, :]
    o_ref[0] = _ln_swish(acc, lg_ref[...], lb_ref[...])
    nh_ref[0] = full_ref[n_new:n_new + nh, :]


def sample_conv(glu, hist, cw, cb, lg, lb):
    nb, n_new, c = glu.shape
    nh = CONV_W - 1
    const = lambda b: (0, 0)
    per_b = lambda b: (b, 0, 0)
    return pl.pallas_call(
        functools.partial(_sample_conv_kernel, n_new=n_new),
        grid=(nb,),
        in_specs=[pl.BlockSpec((1, nh, c), per_b), pl.BlockSpec((1, n_new, c), per_b),
                  pl.BlockSpec((CONV_W, c), const),
                  pl.BlockSpec((1, c), const), pl.BlockSpec((1, c), const), pl.BlockSpec((1, c), const)],
        out_specs=[pl.BlockSpec((1, n_new, c), per_b), pl.BlockSpec((1, nh, c), per_b)],
        out_shape=[jax.ShapeDtypeStruct((nb, n_new, c), F32), jax.ShapeDtypeStruct((nb, nh, c), F32)],
        scratch_shapes=[pltpu.VMEM((nh + n_new + 6, c), F32)],
        compiler_params=_params("parallel"),
        name="sample_conv",
    )(hist, glu, cw, cb, lg, lb)


def _query_columns(qb, nb, n_new):
    w = n_new * N_HEADS
    q = qb.reshape(nb, n_new, N_HEADS, 2, HEAD_DIM).transpose(0, 3, 4, 1, 2).reshape(nb, 2, HEAD_DIM, w)
    zeros = lambda n: jnp.zeros((nb, HEAD_DIM, n), qb.dtype)
    top = jnp.concatenate([q[:, 0], zeros(SA_COLS - w)], axis=-1)
    bot = jnp.concatenate([zeros(w), q[:, 1], zeros(SA_COLS - 2 * w)], axis=-1)
    return jnp.concatenate([top, bot], axis=1)


def _layer(x_p, x_s, ck, cv, hist_s, page_table, w, lam_init, slopes):
    t_p = x_p.shape[0]
    nb, n_new = hist_s.shape[0], x_s.shape[0] // hist_s.shape[0]
    tm_p, tm_s = 512, x_s.shape[0]
    outs = {}
    for name, x, tm, tm_big in (("p", x_p, tm_p, 1024), ("s", x_s, tm_s, tm_s)):
        h = prenorm(x, w["ffn1_pre"], tm)
        a = gateup(h, w["ffn1_wg"], w["ffn1_wu"], tm_big, 512)
        x, h = rowmm([a], [w["ffn1_wd"]], x, w["ffn1_post"], w["mix_pre"], 0.5, tm, 1024)
        qb, k32, v32, kb, vb, glu = in_proj(h, w["w_in"], tm_big, 512, 256)
        if name == "p":
            o = prompt_attention(qb, kb, vb, slopes, w["lam_vecs"], w["subln"], lam_init, 512, 512)
            hist0 = jnp.zeros((HALO, CONV_CH), F32)
            c = prompt_conv(glu, hist0, w["conv_w"], w["conv_b"], w["conv_ln_g"], w["conv_ln_b"], 256)
            new_hist = glu[t_p - (CONV_W - 1):][None]
        else:
            col = jnp.arange(SA_COLS)
            used = col < 2 * n_new * N_HEADS
            scol = jnp.where(used, slopes[col % N_HEADS] * LOG2E, 0.0)[None, :]
            qpos = (page_table.shape[1] * PAGE_SIZE + (col // N_HEADS) % n_new).astype(F32)[None, :]
            o = sample_attention(_query_columns(qb, nb, n_new), scol, qpos, w["lam_vecs"], w["subln"],
                                 k32.reshape(nb, n_new * N_HEADS * 2, HEAD_DIM),
                                 v32.reshape(nb, n_new * N_HEADS, V_DIM),
                                 ck.reshape(-1, HEAD_DIM), cv.reshape(-1, V_DIM),
                                 page_table, lam_init, n_new, 8)
            o = o.reshape(nb * n_new, D_ATTN).astype(BF16)
            c, new_hist = sample_conv(glu.reshape(nb, n_new, CONV_CH), hist_s, w["conv_w"], w["conv_b"],
                                      w["conv_ln_g"], w["conv_ln_b"])
            c = c.reshape(nb * n_new, CONV_CH).astype(BF16)
        x, h = rowmm([o, c], [w["w_out"], w["w_out"]], x, w["mix_post"], w["ffn2_pre"], 1.0, tm, 512,
                     w_row_offsets=(0, D_ATTN))
        a = gateup(h, w["ffn2_wg"], w["ffn2_wu"], tm_big, 512)
        x, _ = rowmm([a], [w["ffn2_wd"]], x, w["ffn2_post"], None, 0.5, tm, 1024)
        outs[name] = (x, k32, v32, new_hist)
    return outs


def kernel(x_prompt, x_sample, cache_k, cache_v, state_conv, page_table, ffn1_pre_g, ffn1_w_gate, ffn1_w_up, ffn1_w_down, ffn1_post_g, mix_pre_g, w_in, lambda_q1, lambda_k1, lambda_q2, lambda_k2, subln_g, conv_w, conv_b, conv_ln_g, conv_ln_b, w_out, mix_post_g, ffn2_pre_g, ffn2_w_gate, ffn2_w_up, ffn2_w_down, ffn2_post_g):
    depth = w_in.shape[0]
    batch, seq, _ = x_prompt.shape
    nb, n_new, _ = x_sample.shape
    slopes = 2.0 ** (-8.0 * jnp.arange(1, N_HEADS + 1, dtype=F32) / N_HEADS)
    up_w = functools.partial(cast_pad_cols, n_pad=D_FF_PAD)
    down_w = functools.partial(cast_pad_rows, r_pad=D_FF_PAD)
    row = lambda v: v.reshape(1, -1).astype(F32)
    assert batch == 1
    x_p = x_prompt.reshape(seq, D_MODEL)
    x_s = x_sample.reshape(nb * n_new, D_MODEL)
    kp, vp, cp, ks, vs, cs = [], [], [], [], [], []
    for l in range(depth):
        wo = w_out[l].astype(BF16)
        w = {
            "ffn1_pre": row(ffn1_pre_g[l]), "ffn1_wg": up_w(ffn1_w_gate[l]), "ffn1_wu": up_w(ffn1_w_up[l]),
            "ffn1_wd": down_w(ffn1_w_down[l]), "ffn1_post": row(ffn1_post_g[l]),
            "mix_pre": row(mix_pre_g[l]), "w_in": w_in[l].astype(BF16), "subln": row(subln_g[l]),
            "lam_vecs": jnp.stack([lambda_q1[l], lambda_k1[l], lambda_q2[l], lambda_k2[l]]).astype(F32),
            "conv_w": conv_w[l].astype(F32), "conv_b": row(conv_b[l]),
            "conv_ln_g": row(conv_ln_g[l]), "conv_ln_b": row(conv_ln_b[l]),
            "w_out": wo, "mix_post": row(mix_post_g[l]),
            "ffn2_pre": row(ffn2_pre_g[l]), "ffn2_wg": up_w(ffn2_w_gate[l]), "ffn2_wu": up_w(ffn2_w_up[l]),
            "ffn2_wd": down_w(ffn2_w_down[l]), "ffn2_post": row(ffn2_post_g[l]),
        }
        outs = _layer(x_p, x_s, cache_k[l], cache_v[l], state_conv[l], page_table, w, _lambda_init(l), slopes)
        x_p, k32, v32, nh = outs["p"]
        kp.append(k32.reshape(batch, seq, N_HEADS, 2, HEAD_DIM))
        vp.append(v32.reshape(batch, seq, N_HEADS, V_DIM))
        cp.append(nh)
        x_s, k32, v32, nh = outs["s"]
        ks.append(k32.reshape(nb, n_new, N_HEADS, 2, HEAD_DIM))
        vs.append(v32.reshape(nb, n_new, N_HEADS, V_DIM))
        cs.append(nh)
    return (x_p.reshape(batch, seq, D_MODEL), x_s.reshape(nb, n_new, D_MODEL),
            jnp.stack(kp), jnp.stack(vp), jnp.stack(cp), jnp.stack(ks), jnp.stack(vs), jnp.stack(cs))
```
